```python
import jax, jax.numpy as jnp
from jax import lax
import numpy as np

D_MODEL = 1024
BATCH = 16
SEQ = 2048
DEPTH = 4

CHUNK = 64
Q_BLOCK = 128
PLE_DIM = 256
FOX_HEADS = 8
FOX_HEAD_DIM = 64
FOX_WIDTH = FOX_HEADS * FOX_HEAD_DIM
RWKV_HEADS = 8
RWKV_HEAD_DIM = 64
RWKV_WIDTH = RWKV_HEADS * RWKV_HEAD_DIM
DECAY_LORA = 64
AAA_LORA = 64
GATE_LORA = 128
VRES_LORA = 32
D_FF = 2816
CONV_WIDTH = 3
N_BRANCH = 2
FOX_COLS = 3 * FOX_WIDTH + FOX_HEADS
RWKV_COLS = 3 * RWKV_WIDTH + DECAY_LORA + AAA_LORA + GATE_LORA
GATE_COLS = N_BRANCH * D_MODEL
N_IN = FOX_COLS + RWKV_COLS + GATE_COLS
RMS_EPS = 1e-6
GN_EPS = 64e-5
NEG_BIG = -1e30

kernel_name = "hybrid_fox_rwkv7_convffn_ple"


def rmsnorm(x, g):
    xf = x.astype(jnp.float32)
    y = xf * lax.rsqrt(jnp.mean(xf * xf, axis=-1, keepdims=True) + RMS_EPS)
    return (y * g.astype(jnp.float32)).astype(x.dtype)


def token_shift(u):
    return jnp.concatenate([jnp.zeros_like(u[:, :1]), u[:, :-1]], axis=1)


def causal_dwconv(u, w, b):
    k_w = w.shape[0]
    s = u.shape[1]
    up = jnp.pad(u, ((0, 0), (k_w - 1, 0), (0, 0)))
    out = b
    for j in range(k_w):
        out = out + up[:, j:j + s] * w[j]
    return out


def fox_attention(q, k, v, log_f):
    s_len = q.shape[1]
    scale = FOX_HEAD_DIM ** -0.5
    c = jnp.cumsum(log_f.astype(jnp.float32), axis=1).transpose(0, 2, 1)
    outs = []
    for start in range(0, s_len, Q_BLOCK):
        end = start + Q_BLOCK
        qb = q[:, start:end]
        kb = k[:, :end]
        vb = v[:, :end]
        sc = jnp.einsum('bqhd,bkhd->bhqk', qb, kb).astype(jnp.float32) * scale
        bias = c[:, :, start:end, None] - c[:, :, None, :end]
        qpos = start + jnp.arange(Q_BLOCK)
        kpos = jnp.arange(end)
        mask = kpos[None, :] <= qpos[:, None]
        sc = jnp.where(mask, sc + bias, NEG_BIG)
        pr = jax.nn.softmax(sc, axis=-1).astype(v.dtype)
        outs.append(jnp.einsum('bhqk,bkhd->bqhd', pr, vb))
    return jnp.concatenate(outs, axis=1)


def rwkv7_scan(r, w, k, v, a, b):
    bsz, _, nh, n = r.shape

    def step(st, inp):
        r_t, w_t, k_t, v_t, a_t, b_t = inp
        sa = jnp.einsum('bhvk,bhk->bhv', st, a_t)
        st = st * w_t[:, :, None, :] + sa[..., None] * b_t[:, :, None, :] + v_t[..., None] * k_t[:, :, None, :]
        y = jnp.einsum('bhvk,bhk->bhv', st, r_t)
        return st, y

    xs = tuple(jnp.moveaxis(t.astype(jnp.float32), 1, 0) for t in (r, w, k, v, a, b))
    s0 = jnp.zeros((bsz, nh, n, n), jnp.float32)
    _, ys = lax.scan(step, s0, xs)
    return jnp.moveaxis(ys, 0, 1)


def setup_inputs(seed: int = 0) -> dict:
    key = jax.random.key(seed)
    ks = jax.random.split(key, 40)

    def nrm(k, shape, scale):
        return jax.random.normal(k, shape, jnp.float32) * scale

    L = DEPTH
    LV = DEPTH - 1
    return {
        "x": nrm(ks[0], (BATCH, SEQ, D_MODEL), 1.0),
        "p": nrm(ks[1], (DEPTH, BATCH, SEQ, PLE_DIM), 1.0),
        "g_mix": 1.0 + nrm(ks[2], (L, D_MODEL), 0.02),
        "w_in": nrm(ks[3], (L, D_MODEL, N_IN), D_MODEL ** -0.5),
        "b_f": 2.0 + nrm(ks[4], (L, FOX_HEADS), 0.5),
        "g_qnorm": 1.0 + nrm(ks[5], (L, FOX_HEAD_DIM), 0.02),
        "g_knorm": 1.0 + nrm(ks[6], (L, FOX_HEAD_DIM), 0.02),
        "mu_shift": jax.random.uniform(ks[7], (L, RWKV_COLS), jnp.float32),
        "w_decay_up": nrm(ks[8], (L, DECAY_LORA, RWKV_WIDTH), 0.1 * DECAY_LORA ** -0.5),
        "w0": nrm(ks[9], (L, RWKV_WIDTH), 0.5),
        "w_aaa_up": nrm(ks[10], (L, AAA_LORA, RWKV_WIDTH), 0.5 * AAA_LORA ** -0.5),
        "a0": nrm(ks[11], (L, RWKV_WIDTH), 0.1),
        "w_gate_up": nrm(ks[12], (L, GATE_LORA, RWKV_WIDTH), GATE_LORA ** -0.5),
        "k_k": 1.0 + nrm(ks[13], (L, RWKV_WIDTH), 0.1),
        "k_a": 1.0 + nrm(ks[14], (L, RWKV_WIDTH), 0.1),
        "r_k": nrm(ks[15], (L, RWKV_HEADS, RWKV_HEAD_DIM), 0.1),
        "gn_g": 1.0 + nrm(ks[16], (L, RWKV_WIDTH), 0.02),
        "gn_b": nrm(ks[17], (L, RWKV_WIDTH), 0.01),
        "w_vres_down": nrm(ks[18], (LV, D_MODEL, VRES_LORA), D_MODEL ** -0.5),
        "w_vres_up": nrm(ks[19], (LV, VRES_LORA, RWKV_WIDTH), 0.5 * VRES_LORA ** -0.5),
        "v0": nrm(ks[20], (LV, RWKV_WIDTH), 0.1),
        "w_o_fox": nrm(ks[21], (L, FOX_WIDTH, D_MODEL), FOX_WIDTH ** -0.5),
        "w_o_rwkv": nrm(ks[22], (L, RWKV_WIDTH, D_MODEL), RWKV_WIDTH ** -0.5),
        "w_out": nrm(ks[23], (L, D_MODEL, D_MODEL), D_MODEL ** -0.5),
        "g_ffn": 1.0 + nrm(ks[24], (L, D_MODEL), 0.02),
        "w_up": nrm(ks[25], (L, D_MODEL, 2 * D_FF), D_MODEL ** -0.5),
        "conv_w": nrm(ks[26], (L, CONV_WIDTH, 2 * D_FF), CONV_WIDTH ** -0.5),
        "conv_b": nrm(ks[27], (L, 2 * D_FF), 0.01),
        "w_down": nrm(ks[28], (L, D_FF, D_MODEL), D_FF ** -0.5),
        "g_ple": 1.0 + nrm(ks[29], (L, D_MODEL), 0.02),
        "w_ple_gate": nrm(ks[30], (L, D_MODEL, D_MODEL), D_MODEL ** -0.5),
        "w_ple_up": nrm(ks[31], (L, PLE_DIM, D_MODEL), PLE_DIM ** -0.5),
    }


def reference(x, p, g_mix, w_in, b_f, g_qnorm, g_knorm, mu_shift, w_decay_up, w0, w_aaa_up, a0,
              w_gate_up, k_k, k_a, r_k, gn_g, gn_b, w_vres_down, w_vres_up, v0, w_o_fox, w_o_rwkv,
              w_out, g_ffn, w_up, conv_w, conv_b, w_down, g_ple, w_ple_gate, w_ple_up):
    bsz, s_len, _ = x.shape
    v_first = None
    for i in range(DEPTH):
        h = rmsnorm(x, g_mix[i])
        z = h @ w_in[i]
        zf = z[..., :FOX_COLS]
        zr = z[..., FOX_COLS:FOX_COLS + RWKV_COLS]
        zg = z[..., FOX_COLS + RWKV_COLS:]

        q = zf[..., :FOX_WIDTH].reshape(bsz, s_len, FOX_HEADS, FOX_HEAD_DIM)
        k = zf[..., FOX_WIDTH:2 * FOX_WIDTH].reshape(bsz, s_len, FOX_HEADS, FOX_HEAD_DIM)
        v = zf[..., 2 * FOX_WIDTH:3 * FOX_WIDTH].reshape(bsz, s_len, FOX_HEADS, FOX_HEAD_DIM)
        log_f = jax.nn.log_sigmoid(zf[..., 3 * FOX_WIDTH:] + b_f[i])
        q = rmsnorm(q, g_qnorm[i])
        k = rmsnorm(k, g_knorm[i])
        y_fox = fox_attention(q, k, v, log_f).reshape(bsz, s_len, FOX_WIDTH)

        zr = zr + mu_shift[i] * (token_shift(zr) - zr)
        o0 = 0
        r = zr[..., o0:o0 + RWKV_WIDTH]; o0 += RWKV_WIDTH
        kr = zr[..., o0:o0 + RWKV_WIDTH]; o0 += RWKV_WIDTH
        vr = zr[..., o0:o0 + RWKV_WIDTH]; o0 += RWKV_WIDTH
        dw = zr[..., o0:o0 + DECAY_LORA]; o0 += DECAY_LORA
        da = zr[..., o0:o0 + AAA_LORA]; o0 += AAA_LORA
        dg = zr[..., o0:o0 + GATE_LORA]
        w_log = -jax.nn.softplus(-(w0[i] + jnp.tanh(dw) @ w_decay_up[i])) - 0.5
        decay = jnp.exp(-jnp.exp(w_log.astype(jnp.float32)))
        a = jax.nn.sigmoid(a0[i] + da @ w_aaa_up[i])
        g = jax.nn.sigmoid(dg) @ w_gate_up[i]
        kk = (kr * k_k[i]).reshape(bsz, s_len, RWKV_HEADS, RWKV_HEAD_DIM).astype(jnp.float32)
        kk = kk / jnp.maximum(jnp.sqrt(jnp.sum(kk * kk, axis=-1, keepdims=True)), 1e-12)
        kr = kr * (1.0 + (a - 1.0) * k_a[i])
        if i == 0:
            v_first = vr
        else:
            vmix = jax.nn.sigmoid(v0[i - 1] + (h @ w_vres_down[i - 1]) @ w_vres_up[i - 1])
            vr = vr + (v_first - vr) * vmix
        hs = (bsz, s_len, RWKV_HEADS, RWKV_HEAD_DIM)
        rh, kh, vh = r.reshape(hs), kr.reshape(hs), vr.reshape(hs)
        ah = a.reshape(hs).astype(jnp.float32)
        yr = rwkv7_scan(rh, decay.reshape(hs), kh, vh, -kk, kk * ah)
        mu = jnp.mean(yr, axis=-1, keepdims=True)
        var = jnp.mean(jnp.square(yr - mu), axis=-1, keepdims=True)
        yr = ((yr - mu) * lax.rsqrt(var + GN_EPS)).reshape(bsz, s_len, RWKV_WIDTH)
        yr = (yr * gn_g[i] + gn_b[i]).astype(x.dtype).reshape(hs)
        bonus = jnp.sum(rh * kh * r_k[i], axis=-1, keepdims=True) * vh
        y_rwkv = (yr + bonus).reshape(bsz, s_len, RWKV_WIDTH) * g

        gate_fox = jax.nn.sigmoid(zg[..., :D_MODEL])
        gate_rwkv = jax.nn.sigmoid(zg[..., D_MODEL:])
        merged = gate_fox * (y_fox @ w_o_fox[i]) + gate_rwkv * (y_rwkv @ w_o_rwkv[i])
        x = x + merged @ w_out[i]

        h2 = rmsnorm(x, g_ffn[i])
        u = causal_dwconv(h2 @ w_up[i], conv_w[i], conv_b[i])
        x = x + (jax.nn.gelu(u[..., :D_FF], approximate=True) * u[..., D_FF:]) @ w_down[i]

        ple_gate = jax.nn.sigmoid(rmsnorm(x, g_ple[i]) @ w_ple_gate[i])
        x = x + ple_gate * (p[i] @ w_ple_up[i])
    return x
```

```python
import functools

import jax
import jax.numpy as jnp
from jax import lax
from jax.experimental import pallas as pl
from jax.experimental.pallas import tpu as pltpu

F32 = jnp.float32
BF16 = jnp.bfloat16

LANES = 128
HEAD_DIM = 64
HEADS = 8
PAIR = 2 * HEAD_DIM
N_PAIRS = HEADS // 2
WIDTH = HEADS * HEAD_DIM
D_MODEL = 1024
D_FF = 2816
PLE_DIM = 256
DECAY_LORA = 64
AAA_LORA = 64
GATE_LORA = 128
VRES_LORA = 32
RMS_EPS = 1e-6
GN_EPS = 64e-5
NEG_BIG = -1e30

COL_GATE_FOX = 0
COL_GATE_RWKV = 8
COL_Q = 16
COL_K = 20
COL_V = 24
COL_R = 28
COL_KR = 32
COL_VR = 36
COL_DWDA = 40
COL_DG = 41
COL_F = 42
COL_VRES = 43
N_CAT = 44 * LANES

IN_TILE = 512
TOKEN_TILE = 512
FF_TILE = 256
CHUNK = 64
ATT_BLOCK = 128
VMEM_LIMIT = 56 * 1024 * 1024

(P_MU_R, P_MU_K, P_MU_V, P_W0, P_A0, P_KK, P_KA, P_RK, P_GN_G, P_GN_B, P_V0) = range(11)
P_ROWS = 16


def _mm(a, b):
    return jnp.dot(a.astype(BF16), b.astype(BF16), preferred_element_type=F32)


def _mm_nt(a, b):
    return lax.dot_general(a.astype(BF16), b.astype(BF16), (((1,), (1,)), ((), ())),
                           preferred_element_type=F32)


def _mm_tn(a, b):
    return lax.dot_general(a.astype(BF16), b.astype(BF16), (((0,), (0,)), ((), ())),
                           preferred_element_type=F32)


def _sigmoid(x):
    return 1.0 / (1.0 + jnp.exp(-x))


def _softplus(x):
    return jnp.maximum(x, 0.0) + jnp.log(1.0 + jnp.exp(-jnp.abs(x)))


def _rms_rows(x, g):
    ms = jnp.mean(x * x, axis=-1, keepdims=True)
    return x * lax.rsqrt(ms + RMS_EPS) * g


def _shift_rows(u, k):
    row = lax.broadcasted_iota(jnp.int32, u.shape, 0)
    return jnp.where(row >= k, pltpu.roll(u, k, axis=0), 0.0)


def _cumsum_rows(x, seg):
    pos = lax.broadcasted_iota(jnp.int32, x.shape, 0)
    if seg != x.shape[0]:
        assert seg & (seg - 1) == 0
        pos = pos & (seg - 1)
    k = 1
    while k < seg:
        x = x + jnp.where(pos >= k, pltpu.roll(x, k, axis=0), 0.0)
        k *= 2
    return x


def _pair_sum(x):
    low = lax.broadcasted_iota(jnp.int32, x.shape, 1) < HEAD_DIM
    s0 = jnp.sum(jnp.where(low, x, 0.0), axis=-1, keepdims=True)
    s1 = jnp.sum(jnp.where(low, 0.0, x), axis=-1, keepdims=True)
    return jnp.where(low, s0, s1)


def _in_proj_kernel(x_ref, g_ref, w_ref, z_ref, h_ref):
    @pl.when(pl.program_id(1) == 0)
    def _():
        h_ref[...] = _rms_rows(x_ref[...], g_ref[...]).astype(BF16)

    z_ref[...] = jnp.dot(h_ref[...], w_ref[...], preferred_element_type=F32)


def _in_proj(x, g, w_cat):
    bsz, s_len, _ = x.shape
    return pl.pallas_call(
        _in_proj_kernel,
        grid=(bsz, N_CAT // IN_TILE),
        in_specs=[
            pl.BlockSpec((None, s_len, D_MODEL), lambda b, j: (b, 0, 0)),
            pl.BlockSpec((1, D_MODEL), lambda b, j: (0, 0)),
            pl.BlockSpec((D_MODEL, IN_TILE), lambda b, j: (0, j)),
        ],
        out_specs=pl.BlockSpec((None, s_len, IN_TILE), lambda b, j: (b, 0, j)),
        out_shape=jax.ShapeDtypeStruct((bsz, s_len, N_CAT), F32),
        scratch_shapes=[pltpu.VMEM((s_len, D_MODEL), BF16)],
        compiler_params=pltpu.CompilerParams(
            dimension_semantics=("arbitrary", "arbitrary"), vmem_limit_bytes=VMEM_LIMIT),
        name="in_proj",
    )(x, g, w_cat)


def _split3(c):
    hi = c.astype(BF16).astype(F32)
    r1 = c - hi
    mid = r1.astype(BF16).astype(F32)
    lo = (r1 - mid).astype(BF16).astype(F32)
    return hi, mid, lo


def _fox_kernel(zq_ref, zk_ref, zv_ref, zf_ref, bf_ref, gq_ref, gk_ref, y_ref, q_s, k_s, v_s, o_s):
    s_len = zq_ref.shape[0]
    n_blocks = s_len // ATT_BLOCK
    pair = pl.program_id(1)
    lane = lax.broadcasted_iota(jnp.int32, (s_len, PAIR), 1)
    low = lane < HEAD_DIM

    def head_norm(x, g):
        ms = _pair_sum(x * x) * (1.0 / HEAD_DIM)
        return x * lax.rsqrt(ms + RMS_EPS) * g

    q = head_norm(zq_ref[...], gq_ref[...]) * (HEAD_DIM ** -0.5)
    k = head_norm(zk_ref[...], gk_ref[...])
    v = zv_ref[...]
    c_all = _cumsum_rows(-_softplus(-(zf_ref[...] + bf_ref[...])), s_len)

    for hh in range(2):
        c = jnp.sum(jnp.where(lane == 2 * pair + hh, c_all, 0.0), axis=-1, keepdims=True)
        hi, mid, lo = _split3(c)
        q_ext = jnp.where(lane == HEAD_DIM, hi, jnp.where(lane == HEAD_DIM + 1, mid, jnp.where(
            lane == HEAD_DIM + 2, lo, jnp.where(lane < HEAD_DIM + 6, 1.0, 0.0))))
        k_ext = jnp.where(lane < HEAD_DIM + 3, 1.0, jnp.where(lane == HEAD_DIM + 3, -hi, jnp.where(
            lane == HEAD_DIM + 4, -mid, jnp.where(lane == HEAD_DIM + 5, -lo, 0.0))))
        qh = q if hh == 0 else pltpu.roll(q, HEAD_DIM, axis=1)
        kh = k if hh == 0 else pltpu.roll(k, HEAD_DIM, axis=1)
        vh = v if hh == 0 else pltpu.roll(v, HEAD_DIM, axis=1)
        q_s[hh] = jnp.where(low, qh, q_ext).astype(BF16)
        k_s[hh] = jnp.where(low, kh, k_ext).astype(BF16)
        v_s[hh] = vh.astype(BF16)

    tri = (lax.broadcasted_iota(jnp.int32, (ATT_BLOCK, ATT_BLOCK), 0)
           >= lax.broadcasted_iota(jnp.int32, (ATT_BLOCK, ATT_BLOCK), 1))

    for hh in range(2):
        def q_block(i, carry, hh=hh):
            q0 = pl.multiple_of(i * ATT_BLOCK, ATT_BLOCK)
            qb = q_s[hh, pl.ds(q0, ATT_BLOCK), :]

            def step(k0, state, masked):
                m, l, acc = state
                kb = k_s[hh, pl.ds(k0, ATT_BLOCK), :]
                vb = v_s[hh, pl.ds(k0, ATT_BLOCK), :]
                s = _mm_nt(qb, kb)
                if masked:
                    s = jnp.where(tri, s, NEG_BIG)
                m_new = jnp.maximum(m, jnp.max(s, axis=-1, keepdims=True))
                p = jnp.exp(s - m_new)
                alpha = jnp.exp(m - m_new)
                l = alpha * l + jnp.sum(p, axis=-1, keepdims=True)
                acc = alpha * acc + _mm(p, vb)
                return m_new, l, acc

            state = (jnp.full((ATT_BLOCK, 1), NEG_BIG, F32), jnp.zeros((ATT_BLOCK, 1), F32),
                     jnp.zeros((ATT_BLOCK, PAIR), F32))
            state = lax.fori_loop(
                0, i, lambda j, st: step(pl.multiple_of(j * ATT_BLOCK, ATT_BLOCK), st, False), state)
            m, l, acc = step(q0, state, True)
            o_s[hh, pl.ds(q0, ATT_BLOCK), :] = acc / l
            return carry

        lax.fori_loop(0, n_blocks, q_block, 0)

    y_ref[...] = jnp.where(low, o_s[0], pltpu.roll(o_s[1], HEAD_DIM, axis=1))


def _fox(z, b_f, g_q, g_k):
    bsz, s_len, _ = z.shape

    def col(c0):
        return pl.BlockSpec((None, s_len, PAIR), lambda b, h, c0=c0: (b, 0, c0 + h))

    vec = pl.BlockSpec((1, PAIR), lambda b, h: (0, 0))
    return pl.pallas_call(
        _fox_kernel,
        grid=(bsz, N_PAIRS),
        in_specs=[col(COL_Q), col(COL_K), col(COL_V),
                  pl.BlockSpec((None, s_len, PAIR), lambda b, h: (b, 0, COL_F)), vec, vec, vec],
        out_specs=pl.BlockSpec((None, s_len, PAIR), lambda b, h: (b, 0, h)),
        out_shape=jax.ShapeDtypeStruct((bsz, s_len, WIDTH), F32),
        scratch_shapes=[pltpu.VMEM((2, s_len, PAIR), BF16), pltpu.VMEM((2, s_len, PAIR), BF16),
                        pltpu.VMEM((2, s_len, PAIR), BF16), pltpu.VMEM((2, s_len, PAIR), F32)],
        compiler_params=pltpu.CompilerParams(
            dimension_semantics=("arbitrary", "arbitrary"), vmem_limit_bytes=VMEM_LIMIT),
        name="fox_attention",
    )(z, z, z, z, b_f, g_q, g_k)


def _rwkv_kernel(first_layer, *refs):
    if first_layer:
        (zr_ref, zk_ref, zv_ref, zdwda_ref, zdg_ref, pv_ref, pm_ref, wd_ref, wa_ref, wg_ref,
         y_ref, vfirst_out_ref, *scratch) = refs
    else:
        (zr_ref, zk_ref, zv_ref, zdwda_ref, zdg_ref, zvres_ref, vfirst_ref, pv_ref, pm_ref, wd_ref,
         wa_ref, wg_ref, wvu_ref, y_ref, *scratch) = refs
    (ar_s, asm_s, bkm_s, vsm_s, v_s, bhkh_s, gcol_s, g_s, h_s, q_s, y0_s, yraw_s, bonus_s, gate_s) = scratch

    s_len = zr_ref.shape[0]
    c_len = CHUNK
    n_chunks = s_len // c_len
    lane = lax.broadcasted_iota(jnp.int32, (s_len, PAIR), 1)
    low = lane < HEAD_DIM

    def prm(row):
        return pv_ref[row:row + 1, :]

    def mix(u, mu):
        return u + mu * (_shift_rows(u, 1) - u)

    r = mix(zr_ref[...], prm(P_MU_R))
    kr = mix(zk_ref[...], prm(P_MU_K))
    vr = mix(zv_ref[...], prm(P_MU_V))
    dwda = mix(zdwda_ref[...], pm_ref[0:1, :])
    dg = mix(zdg_ref[...], pm_ref[1:2, :])

    w_log = -_softplus(-(prm(P_W0) + _mm(jnp.tanh(dwda), wd_ref[...]))) - 0.5
    log_decay = -jnp.exp(w_log)
    a_gate = _sigmoid(prm(P_A0) + _mm(dwda, wa_ref[...]))
    gate_s[...] = _mm(_sigmoid(dg), wg_ref[...])
    kk = kr * prm(P_KK)
    kk = kk / jnp.maximum(jnp.sqrt(_pair_sum(kk * kk)), 1e-12)
    kr = kr * (1.0 + (a_gate - 1.0) * prm(P_KA))
    if first_layer:
        vfirst_out_ref[...] = vr
    else:
        v_mix = _sigmoid(prm(P_V0) + _mm(zvres_ref[...], wvu_ref[...]))
        vr = vr + (vfirst_ref[...] - vr) * v_mix
    bonus_s[...] = _pair_sum(r * kr * prm(P_RK)) * vr

    def chunked(u):
        return u.reshape(n_chunks, c_len, PAIR)

    cum = chunked(_cumsum_rows(log_decay, c_len))
    total = cum[:, c_len - 1:c_len, :]
    low3 = lax.broadcasted_iota(jnp.int32, (n_chunks, c_len, PAIR), 2) < HEAD_DIM
    b_vec = kk * a_gate
    r_t = chunked(r) * jnp.exp(cum)
    a_t = chunked(-kk) * jnp.exp(cum - chunked(log_decay))
    inv = jnp.exp(-cum)
    b_t = chunked(b_vec) * inv
    k_t = chunked(kr) * inv
    tail = jnp.exp(total - cum)
    v3 = chunked(vr)

    def masked_stack(u):
        return jnp.concatenate([jnp.where(low3, u, 0.0), jnp.where(low3, 0.0, u)], axis=1)

    ar_s[...] = jnp.concatenate([a_t, r_t], axis=1).astype(BF16)
    asm_s[...] = masked_stack(a_t).astype(BF16)
    bkm_s[...] = jnp.concatenate([masked_stack(b_t), masked_stack(k_t)], axis=1).astype(BF16)
    vsm_s[...] = masked_stack(v3).astype(BF16)
    v_s[...] = v3.astype(BF16)
    bhkh_s[...] = jnp.concatenate([chunked(b_vec) * tail, chunked(kr) * tail], axis=1).astype(BF16)
    eye = (lax.broadcasted_iota(jnp.int32, (PAIR, PAIR), 0) == lax.broadcasted_iota(jnp.int32, (PAIR, PAIR), 1))
    eye3 = (lax.broadcasted_iota(jnp.int32, (n_chunks, PAIR, PAIR), 1)
            == lax.broadcasted_iota(jnp.int32, (n_chunks, PAIR, PAIR), 2))
    g_c = jnp.exp(total)
    g_diag = jnp.where(eye3, jnp.broadcast_to(g_c, (n_chunks, PAIR, PAIR)), 0.0)
    gcol_s[...] = jnp.broadcast_to(jnp.sum(g_diag, axis=-1, keepdims=True), (n_chunks, PAIR, PAIR))

    row_c = lax.broadcasted_iota(jnp.int32, (c_len, 2 * c_len), 0)
    col_c = lax.broadcasted_iota(jnp.int32, (c_len, 2 * c_len), 1)
    strict0 = (col_c < c_len) & (row_c > col_c)
    strict1 = (col_c >= c_len) & (row_c > col_c - c_len)
    row_w = lax.broadcasted_iota(jnp.int32, (c_len, 4 * c_len), 0)
    col_w = lax.broadcasted_iota(jnp.int32, (c_len, 4 * c_len), 1)
    lower_w = row_w >= (col_w & (c_len - 1))
    same_head = ((lax.broadcasted_iota(jnp.int32, (PAIR, PAIR), 0) < HEAD_DIM)
                 == (lax.broadcasted_iota(jnp.int32, (PAIR, PAIR), 1) < HEAD_DIM))
    eye_f = jnp.where(eye, 1.0, 0.0)
    zeros_cp = jnp.zeros((c_len, PAIR), BF16)
    zeros_2cp = jnp.zeros((2 * c_len, PAIR), BF16)

    def block_diag(x):
        return jnp.concatenate([jnp.where(strict0, x, 0.0), jnp.where(strict1, x, 0.0)], axis=0)

    def chunk_terms(c, carry):
        ar = ar_s[c]
        p = _mm_nt(ar, bkm_s[c])
        p_a, p_r = p[:c_len], p[c_len:]
        l_bd = block_diag(p_a[:, :2 * c_len])
        lk_bd = block_diag(p_a[:, 2 * c_len:])
        m_r = jnp.where(lower_w, p_r, 0.0)
        t_bd = eye_f + l_bd
        l_pow = l_bd
        for _ in range(c_len.bit_length() - 2):
            l_pow = _mm(l_pow, l_pow)
            t_bd = t_bd + _mm(t_bd, l_pow)
        vsm = vsm_s[c]
        lv = _mm(lk_bd, vsm)
        z = _mm(t_bd, jnp.concatenate([asm_s[c], lv.astype(BF16)], axis=1))
        zz = (z[:c_len] + z[c_len:]).astype(BF16)
        rhs4 = jnp.concatenate([zz, jnp.concatenate([zeros_cp, v_s[c]], axis=1)], axis=0)
        gh = _mm_tn(bhkh_s[c], rhs4)
        g_s[c] = jnp.where(same_head, gh[:, :PAIR], 0.0).astype(BF16)
        h_s[c] = jnp.where(same_head, gh[:, PAIR:], 0.0)
        rhs5 = jnp.concatenate(
            [z.astype(BF16), jnp.concatenate([zeros_2cp, vsm], axis=1)], axis=0)
        qy = _mm(m_r, rhs5)
        q_s[c] = (ar[c_len:].astype(F32) + qy[:, :PAIR]).astype(BF16)
        y0_s[c] = qy[:, PAIR:]
        return carry

    lax.fori_loop(0, n_chunks, chunk_terms, 0)

    def recur(c, m):
        mb = m.astype(BF16)
        yraw_s[c] = _mm(q_s[c], mb) + y0_s[c]
        return gcol_s[c] * m + _mm(g_s[c], mb) + h_s[c]

    lax.fori_loop(0, n_chunks, recur, jnp.zeros((PAIR, PAIR), F32))

    y = yraw_s[...].reshape(s_len, PAIR)
    mu = _pair_sum(y) * (1.0 / HEAD_DIM)
    d = y - mu
    var = _pair_sum(d * d) * (1.0 / HEAD_DIM)
    yn = d * lax.rsqrt(var + GN_EPS) * prm(P_GN_G) + prm(P_GN_B)
    y_ref[...] = (yn + bonus_s[...]) * gate_s[...]


def _rwkv(z, v_first, pv, pm, wd, wa, wg, wvu):
    bsz, s_len, _ = z.shape
    first_layer = v_first is None
    n_chunks = s_len // CHUNK

    def col(c0):
        return pl.BlockSpec((None, s_len, PAIR), lambda b, h, c0=c0: (b, 0, c0 + h))

    def fixed(c0):
        return pl.BlockSpec((None, s_len, PAIR), lambda b, h, c0=c0: (b, 0, c0))

    pair_cols = pl.BlockSpec((None, s_len, PAIR), lambda b, h: (b, 0, h))
    w_spec = pl.BlockSpec((PAIR, PAIR), lambda b, h: (0, h))
    in_specs = [col(COL_R), col(COL_KR), col(COL_VR), fixed(COL_DWDA), fixed(COL_DG)]
    args = [z, z, z, z, z]
    if not first_layer:
        in_specs += [fixed(COL_VRES), pair_cols]
        args += [z, v_first]
    in_specs += [pl.BlockSpec((P_ROWS, PAIR), lambda b, h: (0, h)),
                 pl.BlockSpec((8, PAIR), lambda b, h: (0, 0)), w_spec, w_spec, w_spec]
    args += [pv, pm, wd, wa, wg]
    out_shape = [jax.ShapeDtypeStruct((bsz, s_len, WIDTH), F32)]
    out_specs = [pair_cols]
    if first_layer:
        out_shape.append(jax.ShapeDtypeStruct((bsz, s_len, WIDTH), F32))
        out_specs.append(pair_cols)
    else:
        in_specs.append(w_spec)
        args.append(wvu)
    c2, c4 = 2 * CHUNK, 4 * CHUNK
    scratch = [
        pltpu.VMEM((n_chunks, c2, PAIR), BF16),
        pltpu.VMEM((n_chunks, c2, PAIR), BF16),
        pltpu.VMEM((n_chunks, c4, PAIR), BF16),
        pltpu.VMEM((n_chunks, c2, PAIR), BF16),
        pltpu.VMEM((n_chunks, CHUNK, PAIR), BF16),
        pltpu.VMEM((n_chunks, c2, PAIR), BF16),
        pltpu.VMEM((n_chunks, PAIR, PAIR), F32),
        pltpu.VMEM((n_chunks, PAIR, PAIR), BF16),
        pltpu.VMEM((n_chunks, PAIR, PAIR), F32),
        pltpu.VMEM((n_chunks, CHUNK, PAIR), BF16),
        pltpu.VMEM((n_chunks, CHUNK, PAIR), F32),
        pltpu.VMEM((n_chunks, CHUNK, PAIR), F32),
        pltpu.VMEM((s_len, PAIR), F32),
        pltpu.VMEM((s_len, PAIR), F32),
    ]
    out = pl.pallas_call(
        functools.partial(_rwkv_kernel, first_layer),
        grid=(bsz, N_PAIRS),
        in_specs=in_specs,
        out_specs=out_specs,
        out_shape=out_shape,
        scratch_shapes=scratch,
        compiler_params=pltpu.CompilerParams(
            dimension_semantics=("arbitrary", "arbitrary"), vmem_limit_bytes=VMEM_LIMIT),
        name="rwkv_first" if first_layer else "rwkv",
    )(*args)
    return (out[0], out[1]) if first_layer else (out[0], v_first)


def _merge_kernel(x_ref, yf_ref, yr_ref, gf_ref, gr_ref, wf_ref, wr_ref, wo_ref, o_ref):
    merged = (_sigmoid(gf_ref[...]) * _mm(yf_ref[...], wf_ref[...])
              + _sigmoid(gr_ref[...]) * _mm(yr_ref[...], wr_ref[...]))
    o_ref[...] = x_ref[...] + _mm(merged, wo_ref[...])


def _merge(x2, yf2, yr2, z2, w_of, w_or, w_out):
    n_tok = x2.shape[0]
    tile = min(TOKEN_TILE, n_tok)
    gate_blocks = D_MODEL // LANES

    def rows(width, c0=0):
        return pl.BlockSpec((tile, width), lambda i, c0=c0: (i, c0))

    def whole(a):
        return pl.BlockSpec(a.shape, lambda i: (0, 0))

    return pl.pallas_call(
        _merge_kernel,
        grid=(n_tok // tile,),
        in_specs=[rows(D_MODEL), rows(WIDTH), rows(WIDTH), rows(D_MODEL, COL_GATE_FOX // gate_blocks),
                  rows(D_MODEL, COL_GATE_RWKV // gate_blocks), whole(w_of), whole(w_or), whole(w_out)],
        out_specs=rows(D_MODEL),
        out_shape=jax.ShapeDtypeStruct(x2.shape, F32),
        compiler_params=pltpu.CompilerParams(
            dimension_semantics=("arbitrary",), vmem_limit_bytes=VMEM_LIMIT),
        name="merge_out",
    )(x2, yf2, yr2, z2, z2, w_of, w_or, w_out)


def _ffn_kernel(x_ref, g_ref, wu1_ref, wu2_ref, cw1_ref, cw2_ref, cb1_ref, cb2_ref, wd_ref, o_ref, h_ref):
    j = pl.program_id(1)

    @pl.when(j == 0)
    def _():
        x = x_ref[...]
        h_ref[...] = _rms_rows(x, g_ref[...]).astype(BF16)
        o_ref[...] = x

    def conv(u, w_ref, b_ref):
        return (b_ref[...] + _shift_rows(u, 2) * w_ref[0:1, :] + _shift_rows(u, 1) * w_ref[1:2, :]
                + u * w_ref[2:3, :])

    h = h_ref[...]
    u1 = conv(jnp.dot(h, wu1_ref[...], preferred_element_type=F32), cw1_ref, cb1_ref)
    u2 = conv(jnp.dot(h, wu2_ref[...], preferred_element_type=F32), cw2_ref, cb2_ref)
    gelu = 0.5 * u1 * (1.0 + jnp.tanh(0.7978845608028654 * (u1 + 0.044715 * (u1 * u1 * u1))))
    o_ref[...] += _mm(gelu * u2, wd_ref[...])


def _ffn(x, g, w_up, conv_w, conv_b, w_down):
    bsz, s_len, _ = x.shape
    n_ff = D_FF // FF_TILE
    row_spec = pl.BlockSpec((None, s_len, D_MODEL), lambda b, j: (b, 0, 0))

    def cols(rows, off):
        return pl.BlockSpec((rows, FF_TILE), lambda b, j, off=off: (0, off + j))

    return pl.pallas_call(
        _ffn_kernel,
        grid=(bsz, n_ff),
        in_specs=[row_spec, pl.BlockSpec((1, D_MODEL), lambda b, j: (0, 0)),
                  cols(D_MODEL, 0), cols(D_MODEL, n_ff), cols(3, 0), cols(3, n_ff), cols(1, 0), cols(1, n_ff),
                  pl.BlockSpec((FF_TILE, D_MODEL), lambda b, j: (j, 0))],
        out_specs=row_spec,
        out_shape=jax.ShapeDtypeStruct(x.shape, F32),
        scratch_shapes=[pltpu.VMEM((s_len, D_MODEL), BF16)],
        compiler_params=pltpu.CompilerParams(
            dimension_semantics=("arbitrary", "arbitrary"), vmem_limit_bytes=VMEM_LIMIT),
        name="conv_ffn",
    )(x, g, w_up, w_up, conv_w, conv_w, conv_b, conv_b, w_down)


def _ple_kernel(x_ref, p_ref, g_ref, wg_ref, wu_ref, o_ref):
    x = x_ref[...]
    gate = _sigmoid(_mm(_rms_rows(x, g_ref[...]), wg_ref[...]))
    o_ref[...] = x + gate * _mm(p_ref[...], wu_ref[...])


def _ple(x2, p2, g, w_gate, w_up):
    n_tok = x2.shape[0]
    tile = min(TOKEN_TILE, n_tok)
    return pl.pallas_call(
        _ple_kernel,
        grid=(n_tok // tile,),
        in_specs=[pl.BlockSpec((tile, D_MODEL), lambda i: (i, 0)), pl.BlockSpec((tile, PLE_DIM), lambda i: (i, 0)),
                  pl.BlockSpec((1, D_MODEL), lambda i: (0, 0)), pl.BlockSpec(w_gate.shape, lambda i: (0, 0)),
                  pl.BlockSpec(w_up.shape, lambda i: (0, 0))],
        out_specs=pl.BlockSpec((tile, D_MODEL), lambda i: (i, 0)),
        out_shape=jax.ShapeDtypeStruct(x2.shape, F32),
        compiler_params=pltpu.CompilerParams(
            dimension_semantics=("arbitrary",), vmem_limit_bytes=VMEM_LIMIT),
        name="ple",
    )(x2, p2, g, w_gate, w_up)


def _pad_cols(a, width):
    return jnp.pad(a, ((0, 0), (0, width - a.shape[1])))


def _pad_rows(a, top, total):
    return jnp.pad(a, ((top, total - top - a.shape[0]), (0, 0)))


def _cat_weight(w_in, w_vres_down):
    fox = 3 * WIDTH
    rw0 = fox + HEADS
    lora0 = rw0 + 3 * WIDTH
    gate0 = lora0 + DECAY_LORA + AAA_LORA + GATE_LORA
    vres = (jnp.zeros((D_MODEL, LANES), F32) if w_vres_down is None else _pad_cols(w_vres_down, LANES))
    return jnp.concatenate([
        w_in[:, gate0:], w_in[:, :fox], w_in[:, rw0:lora0], w_in[:, lora0:gate0],
        _pad_cols(w_in[:, fox:rw0], LANES), vres], axis=1).astype(BF16)


def kernel(x, p, g_mix, w_in, b_f, g_qnorm, g_knorm, mu_shift, w_decay_up, w0, w_aaa_up, a0, w_gate_up, k_k, k_a, r_k, gn_g, gn_b, w_vres_down, w_vres_up, v0, w_o_fox, w_o_rwkv, w_out, g_ffn, w_up, conv_w, conv_b, w_down, g_ple, w_ple_gate, w_ple_up):
    bsz, s_len, _ = x.shape
    depth = w_in.shape[0]
    n_tok = bsz * s_len
    assert s_len % ATT_BLOCK == 0 and s_len % CHUNK == 0 and n_tok % min(TOKEN_TILE, n_tok) == 0
    v_first = None
    for i in range(depth):
        mu = mu_shift[i]
        zero_row = jnp.zeros((WIDTH,), F32)
        pv = jnp.stack([
            mu[:WIDTH], mu[WIDTH:2 * WIDTH], mu[2 * WIDTH:3 * WIDTH], w0[i], a0[i], k_k[i], k_a[i],
            r_k[i].reshape(WIDTH), gn_g[i], gn_b[i], v0[i - 1] if i else zero_row]
            + [zero_row] * (P_ROWS - 11))
        lora0 = 3 * WIDTH
        pm = jnp.pad(mu[lora0:].reshape(2, PAIR), ((0, 6), (0, 0)))
        wd = _pad_rows(w_decay_up[i], 0, PAIR).astype(BF16)
        wa = _pad_rows(w_aaa_up[i], DECAY_LORA, PAIR).astype(BF16)
        wg = w_gate_up[i].astype(BF16)
        wvu = _pad_rows(w_vres_up[i - 1], 0, PAIR).astype(BF16) if i else None

        z = _in_proj(x, g_mix[i][None], _cat_weight(w_in[i], w_vres_down[i - 1] if i else None))
        y_fox = _fox(z, _pad_cols(b_f[i][None], PAIR), jnp.tile(g_qnorm[i], 2)[None], jnp.tile(g_knorm[i], 2)[None])
        y_rwkv, v_first = _rwkv(z, v_first, pv, pm, wd, wa, wg, wvu)
        x2 = _merge(x.reshape(n_tok, D_MODEL), y_fox.reshape(n_tok, WIDTH), y_rwkv.reshape(n_tok, WIDTH),
                    z.reshape(n_tok, N_CAT), w_o_fox[i].astype(BF16), w_o_rwkv[i].astype(BF16),
                    w_out[i].astype(BF16))
        x = _ffn(x2.reshape(bsz, s_len, D_MODEL), g_ffn[i][None], w_up[i].astype(BF16), conv_w[i],
                 conv_b[i][None], w_down[i].astype(BF16))
        x2 = _ple(x.reshape(n_tok, D_MODEL), p[i].reshape(n_tok, PLE_DIM), g_ple[i][None],
                  w_ple_gate[i].astype(BF16), w_ple_up[i].astype(BF16))
        x = x2.reshape(bsz, s_len, D_MODEL)
    return x
```

```python
import functools

import jax
import jax.numpy as jnp
from jax import lax
from jax.experimental import pallas as pl
from jax.experimental.pallas import tpu as pltpu

F32 = jnp.float32
BF16 = jnp.bfloat16

LANES = 128
HEAD_DIM = 64
HEADS = 8
PAIR = 2 * HEAD_DIM
N_PAIRS = HEADS // 2
WIDTH = HEADS * HEAD_DIM
D_MODEL = 1024
D_FF = 2816
PLE_DIM = 256
DECAY_LORA = 64
AAA_LORA = 64
GATE_LORA = 128
VRES_LORA = 32
RMS_EPS = 1e-6
GN_EPS = 64e-5
NEG_BIG = -1e30

COL_GATE_FOX = 0
COL_GATE_RWKV = 8
COL_Q = 16
COL_K = 20
COL_V = 24
COL_R = 28
COL_KR = 32
COL_VR = 36
COL_DWDA = 40
COL_DG = 41
COL_F = 42
COL_VRES = 43
N_CAT = 44 * LANES

IN_TILE = 512
TOKEN_TILE = 512
FF_TILE = 256
CHUNK = 64
CHUNK_GROUP = 8
ATT_BLOCK = 256
VMEM_LIMIT = 56 * 1024 * 1024

(P_MU_R, P_MU_K, P_MU_V, P_W0, P_A0, P_KK, P_KA, P_RK, P_GN_G, P_GN_B, P_V0) = range(11)
P_ROWS = 16


def _mm(a, b):
    return jnp.dot(a.astype(BF16), b.astype(BF16), preferred_element_type=F32)


def _mm_nt(a, b):
    return lax.dot_general(a.astype(BF16), b.astype(BF16), (((1,), (1,)), ((), ())),
                           preferred_element_type=F32)


def _mm_tn(a, b):
    return lax.dot_general(a.astype(BF16), b.astype(BF16), (((0,), (0,)), ((), ())),
                           preferred_element_type=F32)


def _sigmoid(x):
    return 1.0 / (1.0 + jnp.exp(-x))


def _softplus(x):
    return jnp.maximum(x, 0.0) + jnp.log(1.0 + jnp.exp(-jnp.abs(x)))


def _rms_rows(x, g):
    ms = jnp.mean(x * x, axis=-1, keepdims=True)
    return x * lax.rsqrt(ms + RMS_EPS) * g


def _shift_rows(u, k):
    row = lax.broadcasted_iota(jnp.int32, u.shape, 0)
    return jnp.where(row >= k, pltpu.roll(u, k, axis=0), 0.0)


def _cumsum_rows(x, seg):
    pos = lax.broadcasted_iota(jnp.int32, x.shape, 0)
    if seg != x.shape[0]:
        assert seg & (seg - 1) == 0
        pos = pos & (seg - 1)
    k = 1
    while k < seg:
        x = x + jnp.where(pos >= k, pltpu.roll(x, k, axis=0), 0.0)
        k *= 2
    return x


def _pair_sum(x):
    low = lax.broadcasted_iota(jnp.int32, x.shape, 1) < HEAD_DIM
    s0 = jnp.sum(jnp.where(low, x, 0.0), axis=-1, keepdims=True)
    s1 = jnp.sum(jnp.where(low, 0.0, x), axis=-1, keepdims=True)
    return jnp.where(low, s0, s1)


def _in_proj_kernel(x_ref, g_ref, w_ref, z_ref, h_ref):
    @pl.when(pl.program_id(1) == 0)
    def _():
        h_ref[...] = _rms_rows(x_ref[...], g_ref[...]).astype(BF16)

    z_ref[...] = jnp.dot(h_ref[...], w_ref[...], preferred_element_type=F32)


def _in_proj(x, g, w_cat):
    bsz, s_len, _ = x.shape
    return pl.pallas_call(
        _in_proj_kernel,
        grid=(bsz, N_CAT // IN_TILE),
        in_specs=[
            pl.BlockSpec((None, s_len, D_MODEL), lambda b, j: (b, 0, 0)),
            pl.BlockSpec((1, D_MODEL), lambda b, j: (0, 0)),
            pl.BlockSpec((D_MODEL, IN_TILE), lambda b, j: (0, j)),
        ],
        out_specs=pl.BlockSpec((None, s_len, IN_TILE), lambda b, j: (b, 0, j)),
        out_shape=jax.ShapeDtypeStruct((bsz, s_len, N_CAT), F32),
        scratch_shapes=[pltpu.VMEM((s_len, D_MODEL), BF16)],
        compiler_params=pltpu.CompilerParams(
            dimension_semantics=("arbitrary", "arbitrary"), vmem_limit_bytes=VMEM_LIMIT),
        name="in_proj",
    )(x, g, w_cat)


def _split3(c):
    hi = c.astype(BF16).astype(F32)
    r1 = c - hi
    mid = r1.astype(BF16).astype(F32)
    lo = (r1 - mid).astype(BF16).astype(F32)
    return hi, mid, lo


def _fox_kernel(zq_ref, zk_ref, zv_ref, zf_ref, bf_ref, gq_ref, gk_ref, y_ref, q_s, k_s, v_s):
    s_len = zq_ref.shape[0]
    n_blocks = s_len // ATT_BLOCK
    pair = pl.program_id(1)
    lane = lax.broadcasted_iota(jnp.int32, (s_len, PAIR), 1)
    low = lane < HEAD_DIM

    def head_norm(x, g):
        ms = _pair_sum(x * x) * (1.0 / HEAD_DIM)
        return x * lax.rsqrt(ms + RMS_EPS) * g

    q = head_norm(zq_ref[...], gq_ref[...]) * (HEAD_DIM ** -0.5)
    k = head_norm(zk_ref[...], gk_ref[...])
    v = zv_ref[...]
    c_all = _cumsum_rows(-_softplus(-(zf_ref[...] + bf_ref[...])), s_len)

    for hh in range(2):
        c = jnp.sum(jnp.where(lane == 2 * pair + hh, c_all, 0.0), axis=-1, keepdims=True)
        hi, mid, lo = _split3(c)
        q_ext = jnp.where(lane == HEAD_DIM, hi, jnp.where(lane == HEAD_DIM + 1, mid, jnp.where(
            lane == HEAD_DIM + 2, lo, jnp.where(lane < HEAD_DIM + 6, 1.0, 0.0))))
        k_ext = jnp.where(lane < HEAD_DIM + 3, 1.0, jnp.where(lane == HEAD_DIM + 3, -hi, jnp.where(
            lane == HEAD_DIM + 4, -mid, jnp.where(lane == HEAD_DIM + 5, -lo, 0.0))))
        qh = q if hh == 0 else pltpu.roll(q, HEAD_DIM, axis=1)
        kh = k if hh == 0 else pltpu.roll(k, HEAD_DIM, axis=1)
        vh = v if hh == 0 else pltpu.roll(v, HEAD_DIM, axis=1)
        q_s[hh] = jnp.where(low, qh, q_ext).astype(BF16)
        k_s[hh] = jnp.where(low, kh, k_ext).astype(BF16)
        v_s[hh] = vh.astype(BF16)

    tri = (lax.broadcasted_iota(jnp.int32, (ATT_BLOCK, ATT_BLOCK), 0)
           >= lax.broadcasted_iota(jnp.int32, (ATT_BLOCK, ATT_BLOCK), 1))
    low_blk = lax.broadcasted_iota(jnp.int32, (ATT_BLOCK, PAIR), 1) < HEAD_DIM

    def attend(hh, i):
        q0 = i * ATT_BLOCK
        qb = q_s[hh, q0:q0 + ATT_BLOCK, :]
        s_diag = jnp.where(tri, _mm_nt(qb, k_s[hh, q0:q0 + ATT_BLOCK, :]), NEG_BIG)
        m = jnp.max(s_diag, axis=-1, keepdims=True)
        if i:
            s_past = _mm_nt(qb, k_s[hh, 0:q0, :])
            m = jnp.maximum(m, jnp.max(s_past, axis=-1, keepdims=True))
        p_diag = jnp.exp(s_diag - m)
        l = jnp.sum(p_diag, axis=-1, keepdims=True)
        acc = _mm(p_diag, v_s[hh, q0:q0 + ATT_BLOCK, :])
        if i:
            p_past = jnp.exp(s_past - m)
            l = l + jnp.sum(p_past, axis=-1, keepdims=True)
            acc = acc + _mm(p_past, v_s[hh, 0:q0, :])
        return acc / l

    for i in range(n_blocks):
        q0 = i * ATT_BLOCK
        y_ref[q0:q0 + ATT_BLOCK, :] = jnp.where(
            low_blk, attend(0, i), pltpu.roll(attend(1, i), HEAD_DIM, axis=1))


def _fox(z, b_f, g_q, g_k):
    bsz, s_len, _ = z.shape

    def col(c0):
        return pl.BlockSpec((None, s_len, PAIR), lambda b, h, c0=c0: (b, 0, c0 + h))

    vec = pl.BlockSpec((1, PAIR), lambda b, h: (0, 0))
    return pl.pallas_call(
        _fox_kernel,
        grid=(bsz, N_PAIRS),
        in_specs=[col(COL_Q), col(COL_K), col(COL_V),
                  pl.BlockSpec((None, s_len, PAIR), lambda b, h: (b, 0, COL_F)), vec, vec, vec],
        out_specs=pl.BlockSpec((None, s_len, PAIR), lambda b, h: (b, 0, h)),
        out_shape=jax.ShapeDtypeStruct((bsz, s_len, WIDTH), F32),
        scratch_shapes=[pltpu.VMEM((2, s_len, PAIR), BF16), pltpu.VMEM((2, s_len, PAIR), BF16),
                        pltpu.VMEM((2, s_len, PAIR), BF16)],
        compiler_params=pltpu.CompilerParams(
            dimension_semantics=("arbitrary", "arbitrary"), vmem_limit_bytes=VMEM_LIMIT),
        name="fox_attention",
    )(z, z, z, z, b_f, g_q, g_k)


def _rwkv_kernel(first_layer, *refs):
    if first_layer:
        (zr_ref, zk_ref, zv_ref, zdwda_ref, zdg_ref, pv_ref, pm_ref, wd_ref, wa_ref, wg_ref,
         y_ref, vfirst_out_ref, *scratch) = refs
    else:
        (zr_ref, zk_ref, zv_ref, zdwda_ref, zdg_ref, zvres_ref, vfirst_ref, pv_ref, pm_ref, wd_ref,
         wa_ref, wg_ref, wvu_ref, y_ref, *scratch) = refs
    (ar_s, asm_s, bkm_s, vsm_s, v_s, bhkh_s, gcol_s, g_s, h_s, q_s, y0_s, yraw_s, bonus_s, gate_s) = scratch

    s_len = zr_ref.shape[0]
    c_len = CHUNK
    n_chunks = s_len // c_len
    lane = lax.broadcasted_iota(jnp.int32, (s_len, PAIR), 1)
    low = lane < HEAD_DIM

    def prm(row):
        return pv_ref[row:row + 1, :]

    def mix(u, mu):
        return u + mu * (_shift_rows(u, 1) - u)

    r = mix(zr_ref[...], prm(P_MU_R))
    kr = mix(zk_ref[...], prm(P_MU_K))
    vr = mix(zv_ref[...], prm(P_MU_V))
    dwda = mix(zdwda_ref[...], pm_ref[0:1, :])
    dg = mix(zdg_ref[...], pm_ref[1:2, :])

    w_log = -_softplus(-(prm(P_W0) + _mm(jnp.tanh(dwda), wd_ref[...]))) - 0.5
    log_decay = -jnp.exp(w_log)
    a_gate = _sigmoid(prm(P_A0) + _mm(dwda, wa_ref[...]))
    gate_s[...] = _mm(_sigmoid(dg), wg_ref[...])
    kk = kr * prm(P_KK)
    kk = kk / jnp.maximum(jnp.sqrt(_pair_sum(kk * kk)), 1e-12)
    kr = kr * (1.0 + (a_gate - 1.0) * prm(P_KA))
    if first_layer:
        vfirst_out_ref[...] = vr
    else:
        v_mix = _sigmoid(prm(P_V0) + _mm(zvres_ref[...], wvu_ref[...]))
        vr = vr + (vfirst_ref[...] - vr) * v_mix
    bonus_s[...] = _pair_sum(r * kr * prm(P_RK)) * vr

    def chunked(u):
        return u.reshape(n_chunks, c_len, PAIR)

    cum = chunked(_cumsum_rows(log_decay, c_len))
    total = cum[:, c_len - 1:c_len, :]
    low3 = lax.broadcasted_iota(jnp.int32, (n_chunks, c_len, PAIR), 2) < HEAD_DIM
    b_vec = kk * a_gate
    r_t = chunked(r) * jnp.exp(cum)
    a_t = chunked(-kk) * jnp.exp(cum - chunked(log_decay))
    inv = jnp.exp(-cum)
    b_t = chunked(b_vec) * inv
    k_t = chunked(kr) * inv
    tail = jnp.exp(total - cum)
    v3 = chunked(vr)

    def masked_stack(u):
        return jnp.concatenate([jnp.where(low3, u, 0.0), jnp.where(low3, 0.0, u)], axis=1)

    ar_s[...] = jnp.concatenate([a_t, r_t], axis=1).astype(BF16)
    asm_s[...] = masked_stack(a_t).astype(BF16)
    bkm_s[...] = jnp.concatenate([masked_stack(b_t), masked_stack(k_t)], axis=1).astype(BF16)
    vsm_s[...] = masked_stack(v3).astype(BF16)
    v_s[...] = v3.astype(BF16)
    bhkh_s[...] = jnp.concatenate([chunked(b_vec) * tail, chunked(kr) * tail], axis=1).astype(BF16)
    eye = (lax.broadcasted_iota(jnp.int32, (PAIR, PAIR), 0) == lax.broadcasted_iota(jnp.int32, (PAIR, PAIR), 1))
    eye3 = (lax.broadcasted_iota(jnp.int32, (n_chunks, PAIR, PAIR), 1)
            == lax.broadcasted_iota(jnp.int32, (n_chunks, PAIR, PAIR), 2))
    g_c = jnp.exp(total)
    g_diag = jnp.where(eye3, jnp.broadcast_to(g_c, (n_chunks, PAIR, PAIR)), 0.0)
    gcol_s[...] = jnp.broadcast_to(jnp.sum(g_diag, axis=-1, keepdims=True), (n_chunks, PAIR, PAIR))

    row_c = lax.broadcasted_iota(jnp.int32, (c_len, 2 * c_len), 0)
    col_c = lax.broadcasted_iota(jnp.int32, (c_len, 2 * c_len), 1)
    strict0 = (col_c < c_len) & (row_c > col_c)
    strict1 = (col_c >= c_len) & (row_c > col_c - c_len)
    row_w = lax.broadcasted_iota(jnp.int32, (c_len, 4 * c_len), 0)
    col_w = lax.broadcasted_iota(jnp.int32, (c_len, 4 * c_len), 1)
    lower_w = row_w >= (col_w & (c_len - 1))
    same_head = ((lax.broadcasted_iota(jnp.int32, (PAIR, PAIR), 0) < HEAD_DIM)
                 == (lax.broadcasted_iota(jnp.int32, (PAIR, PAIR), 1) < HEAD_DIM))
    eye_f = jnp.where(eye, 1.0, 0.0)
    zeros_cp = jnp.zeros((c_len, PAIR), BF16)
    zeros_2cp = jnp.zeros((2 * c_len, PAIR), BF16)

    def block_diag(x):
        return jnp.concatenate([jnp.where(strict0, x, 0.0), jnp.where(strict1, x, 0.0)], axis=0)

    group = CHUNK_GROUP if n_chunks % CHUNK_GROUP == 0 else 1

    def chunk_group(cg, carry):
        cs = [cg * group + u for u in range(group)]
        ars = [ar_s[c] for c in cs]
        ps = [_mm_nt(ar, bkm_s[c]) for ar, c in zip(ars, cs)]
        l_pows = [block_diag(p[:c_len, :2 * c_len]) for p in ps]
        ts = [eye_f + l for l in l_pows]
        for _ in range(c_len.bit_length() - 2):
            l_pows = [_mm(l, l) for l in l_pows]
            ts = [t + _mm(t, l) for t, l in zip(ts, l_pows)]
        vsms = [vsm_s[c] for c in cs]
        lvs = [_mm(block_diag(p[:c_len, 2 * c_len:]), vsm) for p, vsm in zip(ps, vsms)]
        zs = [_mm(t, jnp.concatenate([asm_s[c], lv.astype(BF16)], axis=1))
              for t, c, lv in zip(ts, cs, lvs)]
        for c, ar, p, vsm, z in zip(cs, ars, ps, vsms, zs):
            zz = (z[:c_len] + z[c_len:]).astype(BF16)
            rhs4 = jnp.concatenate([zz, jnp.concatenate([zeros_cp, v_s[c]], axis=1)], axis=0)
            gh = _mm_tn(bhkh_s[c], rhs4)
            g_s[c] = jnp.where(same_head, gh[:, :PAIR], 0.0).astype(BF16)
            h_s[c] = jnp.where(same_head, gh[:, PAIR:], 0.0)
            rhs5 = jnp.concatenate(
                [z.astype(BF16), jnp.concatenate([zeros_2cp, vsm], axis=1)], axis=0)
            qy = _mm(jnp.where(lower_w, p[c_len:], 0.0), rhs5)
            q_s[c] = (ar[c_len:].astype(F32) + qy[:, :PAIR]).astype(BF16)
            y0_s[c] = qy[:, PAIR:]
        return carry

    lax.fori_loop(0, n_chunks // group, chunk_group, 0)

    def recur(c, m):
        mb = m.astype(BF16)
        yraw_s[c] = _mm(q_s[c], mb) + y0_s[c]
        return gcol_s[c] * m + _mm(g_s[c], mb) + h_s[c]

    lax.fori_loop(0, n_chunks, recur, jnp.zeros((PAIR, PAIR), F32))

    y = yraw_s[...].reshape(s_len, PAIR)
    mu = _pair_sum(y) * (1.0 / HEAD_DIM)
    d = y - mu
    var = _pair_sum(d * d) * (1.0 / HEAD_DIM)
    yn = d * lax.rsqrt(var + GN_EPS) * prm(P_GN_G) + prm(P_GN_B)
    y_ref[...] = (yn + bonus_s[...]) * gate_s[...]


def _rwkv(z, v_first, pv, pm, wd, wa, wg, wvu):
    bsz, s_len, _ = z.shape
    first_layer = v_first is None
    n_chunks = s_len // CHUNK

    def col(c0):
        return pl.BlockSpec((None, s_len, PAIR), lambda b, h, c0=c0: (b, 0, c0 + h))

    def fixed(c0):
        return pl.BlockSpec((None, s_len, PAIR), lambda b, h, c0=c0: (b, 0, c0))

    pair_cols = pl.BlockSpec((None, s_len, PAIR), lambda b, h: (b, 0, h))
    w_spec = pl.BlockSpec((PAIR, PAIR), lambda b, h: (0, h))
    in_specs = [col(COL_R), col(COL_KR), col(COL_VR), fixed(COL_DWDA), fixed(COL_DG)]
    args = [z, z, z, z, z]
    if not first_layer:
        in_specs += [fixed(COL_VRES), pair_cols]
        args += [z, v_first]
    in_specs += [pl.BlockSpec((P_ROWS, PAIR), lambda b, h: (0, h)),
                 pl.BlockSpec((8, PAIR), lambda b, h: (0, 0)), w_spec, w_spec, w_spec]
    args += [pv, pm, wd, wa, wg]
    out_shape = [jax.ShapeDtypeStruct((bsz, s_len, WIDTH), F32)]
    out_specs = [pair_cols]
    if first_layer:
        out_shape.append(jax.ShapeDtypeStruct((bsz, s_len, WIDTH), F32))
        out_specs.append(pair_cols)
    else:
        in_specs.append(w_spec)
        args.append(wvu)
    c2, c4 = 2 * CHUNK, 4 * CHUNK
    scratch = [
        pltpu.VMEM((n_chunks, c2, PAIR), BF16),
        pltpu.VMEM((n_chunks, c2, PAIR), BF16),
        pltpu.VMEM((n_chunks, c4, PAIR), BF16),
        pltpu.VMEM((n_chunks, c2, PAIR), BF16),
        pltpu.VMEM((n_chunks, CHUNK, PAIR), BF16),
        pltpu.VMEM((n_chunks, c2, PAIR), BF16),
        pltpu.VMEM((n_chunks, PAIR, PAIR), F32),
        pltpu.VMEM((n_chunks, PAIR, PAIR), BF16),
        pltpu.VMEM((n_chunks, PAIR, PAIR), F32),
        pltpu.VMEM((n_chunks, CHUNK, PAIR), BF16),
        pltpu.VMEM((n_chunks, CHUNK, PAIR), F32),
        pltpu.VMEM((n_chunks, CHUNK, PAIR), F32),
        pltpu.VMEM((s_len, PAIR), F32),
        pltpu.VMEM((s_len, PAIR), F32),
    ]
    out = pl.pallas_call(
        functools.partial(_rwkv_kernel, first_layer),
        grid=(bsz, N_PAIRS),
        in_specs=in_specs,
        out_specs=out_specs,
        out_shape=out_shape,
        scratch_shapes=scratch,
        compiler_params=pltpu.CompilerParams(
            dimension_semantics=("arbitrary", "arbitrary"), vmem_limit_bytes=VMEM_LIMIT),
        name="rwkv_first" if first_layer else "rwkv",
    )(*args)
    return (out[0], out[1]) if first_layer else (out[0], v_first)


def _merge_kernel(x_ref, yf_ref, yr_ref, gf_ref, gr_ref, wf_ref, wr_ref, wo_ref, o_ref):
    merged = (_sigmoid(gf_ref[...]) * _mm(yf_ref[...], wf_ref[...])
              + _sigmoid(gr_ref[...]) * _mm(yr_ref[...], wr_ref[...]))
    o_ref[...] = x_ref[...] + _mm(merged, wo_ref[...])


def _merge(x2, yf2, yr2, z2, w_of, w_or, w_out):
    n_tok = x2.shape[0]
    tile = min(TOKEN_TILE, n_tok)
    gate_blocks = D_MODEL // LANES

    def rows(width, c0=0):
        return pl.BlockSpec((tile, width), lambda i, c0=c0: (i, c0))

    def whole(a):
        return pl.BlockSpec(a.shape, lambda i: (0, 0))

    return pl.pallas_call(
        _merge_kernel,
        grid=(n_tok // tile,),
        in_specs=[rows(D_MODEL), rows(WIDTH), rows(WIDTH), rows(D_MODEL, COL_GATE_FOX // gate_blocks),
                  rows(D_MODEL, COL_GATE_RWKV // gate_blocks), whole(w_of), whole(w_or), whole(w_out)],
        out_specs=rows(D_MODEL),
        out_shape=jax.ShapeDtypeStruct(x2.shape, F32),
        compiler_params=pltpu.CompilerParams(
            dimension_semantics=("arbitrary",), vmem_limit_bytes=VMEM_LIMIT),
        name="merge_out",
    )(x2, yf2, yr2, z2, z2, w_of, w_or, w_out)


def _ffn_kernel(x_ref, g_ref, wu1_ref, wu2_ref, cw1_ref, cw2_ref, cb1_ref, cb2_ref, wd_ref, o_ref, h_ref):
    j = pl.program_id(1)

    @pl.when(j == 0)
    def _():
        x = x_ref[...]
        h_ref[...] = _rms_rows(x, g_ref[...]).astype(BF16)
        o_ref[...] = x

    def conv(u, w_ref, b_ref):
        return (b_ref[...] + _shift_rows(u, 2) * w_ref[0:1, :] + _shift_rows(u, 1) * w_ref[1:2, :]
                + u * w_ref[2:3, :])

    h = h_ref[...]
    u1 = conv(jnp.dot(h, wu1_ref[...], preferred_element_type=F32), cw1_ref, cb1_ref)
    u2 = conv(jnp.dot(h, wu2_ref[...], preferred_element_type=F32), cw2_ref, cb2_ref)
    gelu = 0.5 * u1 * (1.0 + jnp.tanh(0.7978845608028654 * (u1 + 0.044715 * (u1 * u1 * u1))))
    o_ref[...] += _mm(gelu * u2, wd_ref[...])


def _ffn(x, g, w_up, conv_w, conv_b, w_down):
    bsz, s_len, _ = x.shape
    n_ff = D_FF // FF_TILE
    row_spec = pl.BlockSpec((None, s_len, D_MODEL), lambda b, j: (b, 0, 0))

    def cols(rows, off):
        return pl.BlockSpec((rows, FF_TILE), lambda b, j, off=off: (0, off + j))

    return pl.pallas_call(
        _ffn_kernel,
        grid=(bsz, n_ff),
        in_specs=[row_spec, pl.BlockSpec((1, D_MODEL), lambda b, j: (0, 0)),
                  cols(D_MODEL, 0), cols(D_MODEL, n_ff), cols(3, 0), cols(3, n_ff), cols(1, 0), cols(1, n_ff),
                  pl.BlockSpec((FF_TILE, D_MODEL), lambda b, j: (j, 0))],
        out_specs=row_spec,
        out_shape=jax.ShapeDtypeStruct(x.shape, F32),
        scratch_shapes=[pltpu.VMEM((s_len, D_MODEL), BF16)],
        compiler_params=pltpu.CompilerParams(
            dimension_semantics=("arbitrary", "arbitrary"), vmem_limit_bytes=VMEM_LIMIT),
        name="conv_ffn",
    )(x, g, w_up, w_up, conv_w, conv_w, conv_b, conv_b, w_down)


def _ple_kernel(x_ref, p_ref, g_ref, wg_ref, wu_ref, o_ref):
    x = x_ref[...]
    gate = _sigmoid(_mm(_rms_rows(x, g_ref[...]), wg_ref[...]))
    o_ref[...] = x + gate * _mm(p_ref[...], wu_ref[...])


def _ple(x2, p2, g, w_gate, w_up):
    n_tok = x2.shape[0]
    tile = min(TOKEN_TILE, n_tok)
    return pl.pallas_call(
        _ple_kernel,
        grid=(n_tok // tile,),
        in_specs=[pl.BlockSpec((tile, D_MODEL), lambda i: (i, 0)), pl.BlockSpec((tile, PLE_DIM), lambda i: (i, 0)),
                  pl.BlockSpec((1, D_MODEL), lambda i: (0, 0)), pl.BlockSpec(w_gate.shape, lambda i: (0, 0)),
                  pl.BlockSpec(w_up.shape, lambda i: (0, 0))],
        out_specs=pl.BlockSpec((tile, D_MODEL), lambda i: (i, 0)),
        out_shape=jax.ShapeDtypeStruct(x2.shape, F32),
        compiler_params=pltpu.CompilerParams(
            dimension_semantics=("arbitrary",), vmem_limit_bytes=VMEM_LIMIT),
        name="ple",
    )(x2, p2, g, w_gate, w_up)


def _pad_cols(a, width):
    return jnp.pad(a, ((0, 0), (0, width - a.shape[1])))


def _pad_rows(a, top, total):
    return jnp.pad(a, ((top, total - top - a.shape[0]), (0, 0)))


def _cat_weight(w_in, w_vres_down):
    fox = 3 * WIDTH
    rw0 = fox + HEADS
    lora0 = rw0 + 3 * WIDTH
    gate0 = lora0 + DECAY_LORA + AAA_LORA + GATE_LORA
    vres = (jnp.zeros((D_MODEL, LANES), F32) if w_vres_down is None else _pad_cols(w_vres_down, LANES))
    return jnp.concatenate([
        w_in[:, gate0:], w_in[:, :fox], w_in[:, rw0:lora0], w_in[:, lora0:gate0],
        _pad_cols(w_in[:, fox:rw0], LANES), vres], axis=1).astype(BF16)


def kernel(x, p, g_mix, w_in, b_f, g_qnorm, g_knorm, mu_shift, w_decay_up, w0, w_aaa_up, a0, w_gate_up, k_k, k_a, r_k, gn_g, gn_b, w_vres_down, w_vres_up, v0, w_o_fox, w_o_rwkv, w_out, g_ffn, w_up, conv_w, conv_b, w_down, g_ple, w_ple_gate, w_ple_up):
    bsz, s_len, _ = x.shape
    depth = w_in.shape[0]
    n_tok = bsz * s_len
    assert s_len % ATT_BLOCK == 0 and s_len % CHUNK == 0 and n_tok % min(TOKEN_TILE, n_tok) == 0
    v_first = None
    for i in range(depth):
        mu = mu_shift[i]
        zero_row = jnp.zeros((WIDTH,), F32)
        pv = jnp.stack([
            mu[:WIDTH], mu[WIDTH:2 * WIDTH], mu[2 * WIDTH:3 * WIDTH], w0[i], a0[i], k_k[i], k_a[i],
            r_k[i].reshape(WIDTH), gn_g[i], gn_b[i], v0[i - 1] if i else zero_row]
            + [zero_row] * (P_ROWS - 11))
        lora0 = 3 * WIDTH
        pm = jnp.pad(mu[lora0:].reshape(2, PAIR), ((0, 6), (0, 0)))
        wd = _pad_rows(w_decay_up[i], 0, PAIR).astype(BF16)
        wa = _pad_rows(w_aaa_up[i], DECAY_LORA, PAIR).astype(BF16)
        wg = w_gate_up[i].astype(BF16)
        wvu = _pad_rows(w_vres_up[i - 1], 0, PAIR).astype(BF16) if i else None

        z = _in_proj(x, g_mix[i][None], _cat_weight(w_in[i], w_vres_down[i - 1] if i else None))
        y_fox = _fox(z, _pad_cols(b_f[i][None], PAIR), jnp.tile(g_qnorm[i], 2)[None], jnp.tile(g_knorm[i], 2)[None])
        y_rwkv, v_first = _rwkv(z, v_first, pv, pm, wd, wa, wg, wvu)
        x2 = _merge(x.reshape(n_tok, D_MODEL), y_fox.reshape(n_tok, WIDTH), y_rwkv.reshape(n_tok, WIDTH),
                    z.reshape(n_tok, N_CAT), w_o_fox[i].astype(BF16), w_o_rwkv[i].astype(BF16),
                    w_out[i].astype(BF16))
        x = _ffn(x2.reshape(bsz, s_len, D_MODEL), g_ffn[i][None], w_up[i].astype(BF16), conv_w[i],
                 conv_b[i][None], w_down[i].astype(BF16))
        x2 = _ple(x.reshape(n_tok, D_MODEL), p[i].reshape(n_tok, PLE_DIM), g_ple[i][None],
                  w_ple_gate[i].astype(BF16), w_ple_up[i].astype(BF16))
        x = x2.reshape(bsz, s_len, D_MODEL)
    return x
```

```python
import functools

import jax
import jax.numpy as jnp
from jax import lax
from jax.experimental import pallas as pl
from jax.experimental.pallas import tpu as pltpu

F32 = jnp.float32
BF16 = jnp.bfloat16

LANES = 128
HEAD_DIM = 64
HEADS = 8
PAIR = 2 * HEAD_DIM
N_PAIRS = HEADS // 2
WIDTH = HEADS * HEAD_DIM
D_MODEL = 1024
D_FF = 2816
PLE_DIM = 256
DECAY_LORA = 64
AAA_LORA = 64
GATE_LORA = 128
VRES_LORA = 32
RMS_EPS = 1e-6
GN_EPS = 64e-5
NEG_BIG = -1e30

COL_GATE_FOX = 0
COL_GATE_RWKV = 8
COL_Q = 16
COL_K = 20
COL_V = 24
COL_R = 28
COL_KR = 32
COL_VR = 36
COL_DWDA = 40
COL_DG = 41
COL_F = 42
COL_VRES = 43
N_CAT = 44 * LANES

IN_TILE = 512
TOKEN_TILE = 512
FF_TILE = 256
FF_ROWS = 512
SUBLANES = 8
GELU_C0 = 0.7978845608028654
GELU_C1 = 0.7978845608028654 * 0.044715
CHUNK = 64
CHUNK_GROUP = 8
ATT_BLOCK = 256
VMEM_LIMIT = 56 * 1024 * 1024

(P_MU_R, P_MU_K, P_MU_V, P_W0, P_A0, P_KK, P_KA, P_RK, P_GN_G, P_GN_B, P_V0) = range(11)
P_ROWS = 16


def _mm(a, b):
    return jnp.dot(a.astype(BF16), b.astype(BF16), preferred_element_type=F32)


def _mm_nt(a, b):
    return lax.dot_general(a.astype(BF16), b.astype(BF16), (((1,), (1,)), ((), ())),
                           preferred_element_type=F32)


def _mm_tn(a, b):
    return lax.dot_general(a.astype(BF16), b.astype(BF16), (((0,), (0,)), ((), ())),
                           preferred_element_type=F32)


def _sigmoid(x):
    return 1.0 / (1.0 + jnp.exp(-x))


def _softplus(x):
    return jnp.maximum(x, 0.0) + jnp.log(1.0 + jnp.exp(-jnp.abs(x)))


def _rms_rows(x, g):
    ms = jnp.mean(x * x, axis=-1, keepdims=True)
    return x * lax.rsqrt(ms + RMS_EPS) * g


def _shift_rows(u, k):
    row = lax.broadcasted_iota(jnp.int32, u.shape, 0)
    return jnp.where(row >= k, pltpu.roll(u, k, axis=0), 0.0)


def _cumsum_rows(x, seg):
    pos = lax.broadcasted_iota(jnp.int32, x.shape, 0)
    if seg != x.shape[0]:
        assert seg & (seg - 1) == 0
        pos = pos & (seg - 1)
    k = 1
    while k < seg:
        x = x + jnp.where(pos >= k, pltpu.roll(x, k, axis=0), 0.0)
        k *= 2
    return x


def _pair_sum(x):
    low = lax.broadcasted_iota(jnp.int32, x.shape, 1) < HEAD_DIM
    s0 = jnp.sum(jnp.where(low, x, 0.0), axis=-1, keepdims=True)
    s1 = jnp.sum(jnp.where(low, 0.0, x), axis=-1, keepdims=True)
    return jnp.where(low, s0, s1)


def _in_proj_kernel(x_ref, g_ref, w_ref, z_ref, h_ref):
    @pl.when(pl.program_id(1) == 0)
    def _():
        h_ref[...] = _rms_rows(x_ref[...], g_ref[...]).astype(BF16)

    z_ref[...] = jnp.dot(h_ref[...], w_ref[...], preferred_element_type=F32)


def _in_proj(x, g, w_cat):
    bsz, s_len, _ = x.shape
    return pl.pallas_call(
        _in_proj_kernel,
        grid=(bsz, N_CAT // IN_TILE),
        in_specs=[
            pl.BlockSpec((None, s_len, D_MODEL), lambda b, j: (b, 0, 0)),
            pl.BlockSpec((1, D_MODEL), lambda b, j: (0, 0)),
            pl.BlockSpec((D_MODEL, IN_TILE), lambda b, j: (0, j)),
        ],
        out_specs=pl.BlockSpec((None, s_len, IN_TILE), lambda b, j: (b, 0, j)),
        out_shape=jax.ShapeDtypeStruct((bsz, s_len, N_CAT), F32),
        scratch_shapes=[pltpu.VMEM((s_len, D_MODEL), BF16)],
        compiler_params=pltpu.CompilerParams(
            dimension_semantics=("arbitrary", "arbitrary"), vmem_limit_bytes=VMEM_LIMIT),
        name="in_proj",
    )(x, g, w_cat)


def _split3(c):
    hi = c.astype(BF16).astype(F32)
    r1 = c - hi
    mid = r1.astype(BF16).astype(F32)
    lo = (r1 - mid).astype(BF16).astype(F32)
    return hi, mid, lo


def _fox_kernel(zq_ref, zk_ref, zv_ref, zf_ref, bf_ref, gq_ref, gk_ref, y_ref, q_s, k_s, v_s):
    s_len = zq_ref.shape[0]
    n_blocks = s_len // ATT_BLOCK
    pair = pl.program_id(1)
    lane = lax.broadcasted_iota(jnp.int32, (s_len, PAIR), 1)
    low = lane < HEAD_DIM

    def head_norm(x, g):
        ms = _pair_sum(x * x) * (1.0 / HEAD_DIM)
        return x * lax.rsqrt(ms + RMS_EPS) * g

    q = head_norm(zq_ref[...], gq_ref[...]) * (HEAD_DIM ** -0.5)
    k = head_norm(zk_ref[...], gk_ref[...])
    v = zv_ref[...]
    c_all = _cumsum_rows(-_softplus(-(zf_ref[...] + bf_ref[...])), s_len)

    for hh in range(2):
        c = jnp.sum(jnp.where(lane == 2 * pair + hh, c_all, 0.0), axis=-1, keepdims=True)
        hi, mid, lo = _split3(c)
        q_ext = jnp.where(lane == HEAD_DIM, hi, jnp.where(lane == HEAD_DIM + 1, mid, jnp.where(
            lane == HEAD_DIM + 2, lo, jnp.where(lane < HEAD_DIM + 6, 1.0, 0.0))))
        k_ext = jnp.where(lane < HEAD_DIM + 3, 1.0, jnp.where(lane == HEAD_DIM + 3, -hi, jnp.where(
            lane == HEAD_DIM + 4, -mid, jnp.where(lane == HEAD_DIM + 5, -lo, 0.0))))
        qh = q if hh == 0 else pltpu.roll(q, HEAD_DIM, axis=1)
        kh = k if hh == 0 else pltpu.roll(k, HEAD_DIM, axis=1)
        vh = v if hh == 0 else pltpu.roll(v, HEAD_DIM, axis=1)
        q_s[hh] = jnp.where(low, qh, q_ext).astype(BF16)
        k_s[hh] = jnp.where(low, kh, k_ext).astype(BF16)
        v_s[hh] = vh.astype(BF16)

    tri = (lax.broadcasted_iota(jnp.int32, (ATT_BLOCK, ATT_BLOCK), 0)
           >= lax.broadcasted_iota(jnp.int32, (ATT_BLOCK, ATT_BLOCK), 1))
    low_blk = lax.broadcasted_iota(jnp.int32, (ATT_BLOCK, PAIR), 1) < HEAD_DIM

    def attend(hh, i):
        q0 = i * ATT_BLOCK
        qb = q_s[hh, q0:q0 + ATT_BLOCK, :]
        s_diag = jnp.where(tri, _mm_nt(qb, k_s[hh, q0:q0 + ATT_BLOCK, :]), NEG_BIG)
        m = jnp.max(s_diag, axis=-1, keepdims=True)
        if i:
            s_past = _mm_nt(qb, k_s[hh, 0:q0, :])
            m = jnp.maximum(m, jnp.max(s_past, axis=-1, keepdims=True))
        p_diag = jnp.exp(s_diag - m)
        l = jnp.sum(p_diag, axis=-1, keepdims=True)
        acc = _mm(p_diag, v_s[hh, q0:q0 + ATT_BLOCK, :])
        if i:
            p_past = jnp.exp(s_past - m)
            l = l + jnp.sum(p_past, axis=-1, keepdims=True)
            acc = acc + _mm(p_past, v_s[hh, 0:q0, :])
        return acc / l

    for i in range(n_blocks):
        q0 = i * ATT_BLOCK
        y_ref[q0:q0 + ATT_BLOCK, :] = jnp.where(
            low_blk, attend(0, i), pltpu.roll(attend(1, i), HEAD_DIM, axis=1))


def _fox(z, b_f, g_q, g_k):
    bsz, s_len, _ = z.shape

    def col(c0):
        return pl.BlockSpec((None, s_len, PAIR), lambda b, h, c0=c0: (b, 0, c0 + h))

    vec = pl.BlockSpec((1, PAIR), lambda b, h: (0, 0))
    return pl.pallas_call(
        _fox_kernel,
        grid=(bsz, N_PAIRS),
        in_specs=[col(COL_Q), col(COL_K), col(COL_V),
                  pl.BlockSpec((None, s_len, PAIR), lambda b, h: (b, 0, COL_F)), vec, vec, vec],
        out_specs=pl.BlockSpec((None, s_len, PAIR), lambda b, h: (b, 0, h)),
        out_shape=jax.ShapeDtypeStruct((bsz, s_len, WIDTH), F32),
        scratch_shapes=[pltpu.VMEM((2, s_len, PAIR), BF16), pltpu.VMEM((2, s_len, PAIR), BF16),
                        pltpu.VMEM((2, s_len, PAIR), BF16)],
        compiler_params=pltpu.CompilerParams(
            dimension_semantics=("arbitrary", "arbitrary"), vmem_limit_bytes=VMEM_LIMIT),
        name="fox_attention",
    )(z, z, z, z, b_f, g_q, g_k)


def _rwkv_kernel(first_layer, *refs):
    if first_layer:
        (zr_ref, zk_ref, zv_ref, zdwda_ref, zdg_ref, pv_ref, pm_ref, wd_ref, wa_ref, wg_ref,
         y_ref, vfirst_out_ref, *scratch) = refs
    else:
        (zr_ref, zk_ref, zv_ref, zdwda_ref, zdg_ref, zvres_ref, vfirst_ref, pv_ref, pm_ref, wd_ref,
         wa_ref, wg_ref, wvu_ref, y_ref, *scratch) = refs
    (ar_s, asm_s, bkm_s, vsm_s, v_s, bhkh_s, gcol_s, g_s, h_s, q_s, y0_s, yraw_s, bonus_s, gate_s) = scratch

    s_len = zr_ref.shape[0]
    c_len = CHUNK
    n_chunks = s_len // c_len
    lane = lax.broadcasted_iota(jnp.int32, (s_len, PAIR), 1)
    low = lane < HEAD_DIM

    def prm(row):
        return pv_ref[row:row + 1, :]

    def mix(u_ref, mu):
        u = u_ref[...]
        prev = jnp.concatenate(
            [_shift_rows(u_ref[:SUBLANES, :], 1), u_ref[SUBLANES - 1:s_len - 1, :]], axis=0)
        return u + mu * (prev - u)

    r = mix(zr_ref, prm(P_MU_R))
    kr = mix(zk_ref, prm(P_MU_K))
    vr = mix(zv_ref, prm(P_MU_V))
    dwda = mix(zdwda_ref, pm_ref[0:1, :])
    dg = mix(zdg_ref, pm_ref[1:2, :])

    w_log = -_softplus(-(prm(P_W0) + _mm(jnp.tanh(dwda), wd_ref[...]))) - 0.5
    log_decay = -jnp.exp(w_log)
    a_gate = _sigmoid(prm(P_A0) + _mm(dwda, wa_ref[...]))
    gate_s[...] = _mm(_sigmoid(dg), wg_ref[...])
    kk = kr * prm(P_KK)
    kk = kk / jnp.maximum(jnp.sqrt(_pair_sum(kk * kk)), 1e-12)
    kr = kr * (1.0 + (a_gate - 1.0) * prm(P_KA))
    if first_layer:
        vfirst_out_ref[...] = vr
    else:
        v_mix = _sigmoid(prm(P_V0) + _mm(zvres_ref[...], wvu_ref[...]))
        vr = vr + (vfirst_ref[...] - vr) * v_mix
    bonus_s[...] = _pair_sum(r * kr * prm(P_RK)) * vr

    def chunked(u):
        return u.reshape(n_chunks, c_len, PAIR)

    cum = chunked(_cumsum_rows(log_decay, c_len))
    total = cum[:, c_len - 1:c_len, :]
    low3 = lax.broadcasted_iota(jnp.int32, (n_chunks, c_len, PAIR), 2) < HEAD_DIM
    b_vec = kk * a_gate
    r_t = chunked(r) * jnp.exp(cum)
    a_t = chunked(-kk) * jnp.exp(cum - chunked(log_decay))
    inv = jnp.exp(-cum)
    b_t = chunked(b_vec) * inv
    k_t = chunked(kr) * inv
    tail = jnp.exp(total - cum)
    v3 = chunked(vr)

    def masked_stack(u):
        return jnp.concatenate([jnp.where(low3, u, 0.0), jnp.where(low3, 0.0, u)], axis=1)

    ar_s[...] = jnp.concatenate([a_t, r_t], axis=1).astype(BF16)
    asm_s[...] = masked_stack(a_t).astype(BF16)
    bkm_s[...] = jnp.concatenate([masked_stack(b_t), masked_stack(k_t)], axis=1).astype(BF16)
    vsm_s[...] = masked_stack(v3).astype(BF16)
    v_s[...] = v3.astype(BF16)
    bhkh_s[...] = jnp.concatenate([chunked(b_vec) * tail, chunked(kr) * tail], axis=1).astype(BF16)
    eye = (lax.broadcasted_iota(jnp.int32, (PAIR, PAIR), 0) == lax.broadcasted_iota(jnp.int32, (PAIR, PAIR), 1))
    eye3 = (lax.broadcasted_iota(jnp.int32, (n_chunks, PAIR, PAIR), 1)
            == lax.broadcasted_iota(jnp.int32, (n_chunks, PAIR, PAIR), 2))
    g_c = jnp.exp(total)
    g_diag = jnp.where(eye3, jnp.broadcast_to(g_c, (n_chunks, PAIR, PAIR)), 0.0)
    gcol_s[...] = jnp.broadcast_to(jnp.sum(g_diag, axis=-1, keepdims=True), (n_chunks, PAIR, PAIR))

    row_c = lax.broadcasted_iota(jnp.int32, (c_len, 2 * c_len), 0)
    col_c = lax.broadcasted_iota(jnp.int32, (c_len, 2 * c_len), 1)
    strict0 = (col_c < c_len) & (row_c > col_c)
    strict1 = (col_c >= c_len) & (row_c > col_c - c_len)
    row_w = lax.broadcasted_iota(jnp.int32, (c_len, 4 * c_len), 0)
    col_w = lax.broadcasted_iota(jnp.int32, (c_len, 4 * c_len), 1)
    lower_w = row_w >= (col_w & (c_len - 1))
    same_head = ((lax.broadcasted_iota(jnp.int32, (PAIR, PAIR), 0) < HEAD_DIM)
                 == (lax.broadcasted_iota(jnp.int32, (PAIR, PAIR), 1) < HEAD_DIM))
    eye_f = jnp.where(eye, 1.0, 0.0)
    zeros_cp = jnp.zeros((c_len, PAIR), BF16)
    zeros_2cp = jnp.zeros((2 * c_len, PAIR), BF16)

    def block_diag(x):
        return jnp.concatenate([jnp.where(strict0, x, 0.0), jnp.where(strict1, x, 0.0)], axis=0)

    group = CHUNK_GROUP if n_chunks % CHUNK_GROUP == 0 else 1

    def chunk_terms(cs):
        ars = [ar_s[c] for c in cs]
        ps = [_mm_nt(ar, bkm_s[c]) for ar, c in zip(ars, cs)]
        l_pows = [block_diag(p[:c_len, :2 * c_len]) for p in ps]
        ts = [eye_f + l for l in l_pows]
        yield
        for _ in range(c_len.bit_length() - 2):
            l_pows = [_mm(l, l) for l in l_pows]
            ts = [t + _mm(t, l) for t, l in zip(ts, l_pows)]
            yield
        vsms = [vsm_s[c] for c in cs]
        lvs = [_mm(block_diag(p[:c_len, 2 * c_len:]), vsm) for p, vsm in zip(ps, vsms)]
        yield
        zs = [_mm(t, jnp.concatenate([asm_s[c], lv.astype(BF16)], axis=1))
              for t, c, lv in zip(ts, cs, lvs)]
        yield
        for c, ar, p, vsm, z in zip(cs, ars, ps, vsms, zs):
            zz = (z[:c_len] + z[c_len:]).astype(BF16)
            rhs4 = jnp.concatenate([zz, jnp.concatenate([zeros_cp, v_s[c]], axis=1)], axis=0)
            gh = _mm_tn(bhkh_s[c], rhs4)
            g_s[c] = jnp.where(same_head, gh[:, :PAIR], 0.0).astype(BF16)
            h_s[c] = jnp.where(same_head, gh[:, PAIR:], 0.0)
            rhs5 = jnp.concatenate(
                [z.astype(BF16), jnp.concatenate([zeros_2cp, vsm], axis=1)], axis=0)
            qy = _mm(jnp.where(lower_w, p[c_len:], 0.0), rhs5)
            q_s[c] = (ar[c_len:].astype(F32) + qy[:, :PAIR]).astype(BF16)
            y0_s[c] = qy[:, PAIR:]

    def recur(c, m):
        mb = m.astype(BF16)
        yraw_s[c] = _mm(q_s[c], mb) + y0_s[c]
        return gcol_s[c] * m + _mm(g_s[c], mb) + h_s[c]

    m = jnp.zeros((PAIR, PAIR), F32)
    pending = []
    for cg in range(n_chunks // group):
        cs = list(range(cg * group, (cg + 1) * group))
        for _ in chunk_terms(cs):
            if pending:
                m = recur(pending.pop(0), m)
        while pending:
            m = recur(pending.pop(0), m)
        pending = cs
    for c in pending:
        m = recur(c, m)

    y = yraw_s[...].reshape(s_len, PAIR)
    mu = _pair_sum(y) * (1.0 / HEAD_DIM)
    d = y - mu
    var = _pair_sum(d * d) * (1.0 / HEAD_DIM)
    yn = d * lax.rsqrt(var + GN_EPS) * prm(P_GN_G) + prm(P_GN_B)
    y_ref[...] = (yn + bonus_s[...]) * gate_s[...]


def _rwkv(z, v_first, pv, pm, wd, wa, wg, wvu):
    bsz, s_len, _ = z.shape
    first_layer = v_first is None
    n_chunks = s_len // CHUNK

    def col(c0):
        return pl.BlockSpec((None, s_len, PAIR), lambda b, h, c0=c0: (b, 0, c0 + h))

    def fixed(c0):
        return pl.BlockSpec((None, s_len, PAIR), lambda b, h, c0=c0: (b, 0, c0))

    pair_cols = pl.BlockSpec((None, s_len, PAIR), lambda b, h: (b, 0, h))
    w_spec = pl.BlockSpec((PAIR, PAIR), lambda b, h: (0, h))
    in_specs = [col(COL_R), col(COL_KR), col(COL_VR), fixed(COL_DWDA), fixed(COL_DG)]
    args = [z, z, z, z, z]
    if not first_layer:
        in_specs += [fixed(COL_VRES), pair_cols]
        args += [z, v_first]
    in_specs += [pl.BlockSpec((P_ROWS, PAIR), lambda b, h: (0, h)),
                 pl.BlockSpec((8, PAIR), lambda b, h: (0, 0)), w_spec, w_spec, w_spec]
    args += [pv, pm, wd, wa, wg]
    out_shape = [jax.ShapeDtypeStruct((bsz, s_len, WIDTH), F32)]
    out_specs = [pair_cols]
    if first_layer:
        out_shape.append(jax.ShapeDtypeStruct((bsz, s_len, WIDTH), F32))
        out_specs.append(pair_cols)
    else:
        in_specs.append(w_spec)
        args.append(wvu)
    c2, c4 = 2 * CHUNK, 4 * CHUNK
    scratch = [
        pltpu.VMEM((n_chunks, c2, PAIR), BF16),
        pltpu.VMEM((n_chunks, c2, PAIR), BF16),
        pltpu.VMEM((n_chunks, c4, PAIR), BF16),
        pltpu.VMEM((n_chunks, c2, PAIR), BF16),
        pltpu.VMEM((n_chunks, CHUNK, PAIR), BF16),
        pltpu.VMEM((n_chunks, c2, PAIR), BF16),
        pltpu.VMEM((n_chunks, PAIR, PAIR), F32),
        pltpu.VMEM((n_chunks, PAIR, PAIR), BF16),
        pltpu.VMEM((n_chunks, PAIR, PAIR), F32),
        pltpu.VMEM((n_chunks, CHUNK, PAIR), BF16),
        pltpu.VMEM((n_chunks, CHUNK, PAIR), F32),
        pltpu.VMEM((n_chunks, CHUNK, PAIR), F32),
        pltpu.VMEM((s_len, PAIR), F32),
        pltpu.VMEM((s_len, PAIR), F32),
    ]
    out = pl.pallas_call(
        functools.partial(_rwkv_kernel, first_layer),
        grid=(bsz, N_PAIRS),
        in_specs=in_specs,
        out_specs=out_specs,
        out_shape=out_shape,
        scratch_shapes=scratch,
        compiler_params=pltpu.CompilerParams(
            dimension_semantics=("arbitrary", "arbitrary"), vmem_limit_bytes=VMEM_LIMIT),
        name="rwkv_first" if first_layer else "rwkv",
    )(*args)
    return (out[0], out[1]) if first_layer else (out[0], v_first)


def _merge_kernel(x_ref, yf_ref, yr_ref, gf_ref, gr_ref, wf_ref, wr_ref, wo_ref, o_ref):
    merged = (_sigmoid(gf_ref[...]) * _mm(yf_ref[...], wf_ref[...])
              + _sigmoid(gr_ref[...]) * _mm(yr_ref[...], wr_ref[...]))
    o_ref[...] = x_ref[...] + _mm(merged, wo_ref[...])


def _merge(x2, yf2, yr2, z2, w_of, w_or, w_out):
    n_tok = x2.shape[0]
    tile = min(TOKEN_TILE, n_tok)
    gate_blocks = D_MODEL // LANES

    def rows(width, c0=0):
        return pl.BlockSpec((tile, width), lambda i, c0=c0: (i, c0))

    def whole(a):
        return pl.BlockSpec(a.shape, lambda i: (0, 0))

    return pl.pallas_call(
        _merge_kernel,
        grid=(n_tok // tile,),
        in_specs=[rows(D_MODEL), rows(WIDTH), rows(WIDTH), rows(D_MODEL, COL_GATE_FOX // gate_blocks),
                  rows(D_MODEL, COL_GATE_RWKV // gate_blocks), whole(w_of), whole(w_or), whole(w_out)],
        out_specs=rows(D_MODEL),
        out_shape=jax.ShapeDtypeStruct(x2.shape, F32),
        compiler_params=pltpu.CompilerParams(
            dimension_semantics=("arbitrary",), vmem_limit_bytes=VMEM_LIMIT),
        name="merge_out",
    )(x2, yf2, yr2, z2, z2, w_of, w_or, w_out)


def _ffn_kernel(x_ref, g_ref, wu1_ref, wu2_ref, cw1_ref, cw2_ref, cb1_ref, cb2_ref, wd_ref, o_ref, h_ref,
                u1_s, u2_s):
    j = pl.program_id(1)
    n_rb, halo_rows, _ = u1_s.shape
    rb = halo_rows - SUBLANES

    @pl.when(j == 0)
    def _():
        x = x_ref[...]
        h_ref[...] = _rms_rows(x, g_ref[...]).astype(BF16)
        o_ref[...] = x

    def up(k):
        h = h_ref[k * rb:(k + 1) * rb, :]
        return (jnp.dot(h, wu1_ref[...], preferred_element_type=F32),
                jnp.dot(h, wu2_ref[...], preferred_element_type=F32))

    def conv(k, u, u_s, w_ref, b_ref):
        u_s[k, SUBLANES:, :] = u
        u_s[k, :SUBLANES, :] = u_s[k - 1, rb:, :] if k else jnp.zeros((SUBLANES, u.shape[1]), F32)
        return (b_ref[...] + u_s[k, SUBLANES - 2:SUBLANES - 2 + rb, :] * w_ref[0:1, :]
                + u_s[k, SUBLANES - 1:SUBLANES - 1 + rb, :] * w_ref[1:2, :] + u * w_ref[2:3, :])

    u_now = up(0)
    for k in range(n_rb):
        u_next = up(k + 1) if k + 1 < n_rb else None
        c1 = conv(k, u_now[0], u1_s, cw1_ref, cb1_ref)
        c2 = conv(k, u_now[1], u2_s, cw2_ref, cb2_ref)
        gelu = 0.5 * c1 * (1.0 + jnp.tanh(c1 * (GELU_C0 + GELU_C1 * (c1 * c1))))
        o_ref[k * rb:(k + 1) * rb, :] += _mm(gelu * c2, wd_ref[...])
        u_now = u_next


def _ffn(x, g, w_up, conv_w, conv_b, w_down):
    bsz, s_len, _ = x.shape
    n_ff = D_FF // FF_TILE
    rb = min(FF_ROWS, s_len)
    assert s_len % rb == 0
    row_spec = pl.BlockSpec((None, s_len, D_MODEL), lambda b, j: (b, 0, 0))

    def cols(rows, off):
        return pl.BlockSpec((rows, FF_TILE), lambda b, j, off=off: (0, off + j))

    return pl.pallas_call(
        _ffn_kernel,
        grid=(bsz, n_ff),
        in_specs=[row_spec, pl.BlockSpec((1, D_MODEL), lambda b, j: (0, 0)),
                  cols(D_MODEL, 0), cols(D_MODEL, n_ff), cols(3, 0), cols(3, n_ff), cols(1, 0), cols(1, n_ff),
                  pl.BlockSpec((FF_TILE, D_MODEL), lambda b, j: (j, 0))],
        out_specs=row_spec,
        out_shape=jax.ShapeDtypeStruct(x.shape, F32),
        scratch_shapes=[pltpu.VMEM((s_len, D_MODEL), BF16),
                        pltpu.VMEM((s_len // rb, rb + SUBLANES, FF_TILE), F32),
                        pltpu.VMEM((s_len // rb, rb + SUBLANES, FF_TILE), F32)],
        compiler_params=pltpu.CompilerParams(
            dimension_semantics=("arbitrary", "arbitrary"), vmem_limit_bytes=VMEM_LIMIT),
        name="conv_ffn",
    )(x, g, w_up, w_up, conv_w, conv_w, conv_b, conv_b, w_down)


def _ple_kernel(x_ref, p_ref, g_ref, wg_ref, wu_ref, o_ref):
    x = x_ref[...]
    gate = _sigmoid(_mm(_rms_rows(x, g_ref[...]), wg_ref[...]))
    o_ref[...] = x + gate * _mm(p_ref[...], wu_ref[...])


def _ple(x2, p2, g, w_gate, w_up):
    n_tok = x2.shape[0]
    tile = min(TOKEN_TILE, n_tok)
    return pl.pallas_call(
        _ple_kernel,
        grid=(n_tok // tile,),
        in_specs=[pl.BlockSpec((tile, D_MODEL), lambda i: (i, 0)), pl.BlockSpec((tile, PLE_DIM), lambda i: (i, 0)),
                  pl.BlockSpec((1, D_MODEL), lambda i: (0, 0)), pl.BlockSpec(w_gate.shape, lambda i: (0, 0)),
                  pl.BlockSpec(w_up.shape, lambda i: (0, 0))],
        out_specs=pl.BlockSpec((tile, D_MODEL), lambda i: (i, 0)),
        out_shape=jax.ShapeDtypeStruct(x2.shape, F32),
        compiler_params=pltpu.CompilerParams(
            dimension_semantics=("arbitrary",), vmem_limit_bytes=VMEM_LIMIT),
        name="ple",
    )(x2, p2, g, w_gate, w_up)


def _pad_cols(a, width):
    return jnp.pad(a, ((0, 0), (0, width - a.shape[1])))


def _pad_rows(a, top, total):
    return jnp.pad(a, ((top, total - top - a.shape[0]), (0, 0)))


def _cat_weight(w_in, w_vres_down):
    fox = 3 * WIDTH
    rw0 = fox + HEADS
    lora0 = rw0 + 3 * WIDTH
    gate0 = lora0 + DECAY_LORA + AAA_LORA + GATE_LORA
    vres = (jnp.zeros((D_MODEL, LANES), F32) if w_vres_down is None else _pad_cols(w_vres_down, LANES))
    return jnp.concatenate([
        w_in[:, gate0:], w_in[:, :fox], w_in[:, rw0:lora0], w_in[:, lora0:gate0],
        _pad_cols(w_in[:, fox:rw0], LANES), vres], axis=1).astype(BF16)


def kernel(x, p, g_mix, w_in, b_f, g_qnorm, g_knorm, mu_shift, w_decay_up, w0, w_aaa_up, a0, w_gate_up, k_k, k_a, r_k, gn_g, gn_b, w_vres_down, w_vres_up, v0, w_o_fox, w_o_rwkv, w_out, g_ffn, w_up, conv_w, conv_b, w_down, g_ple, w_ple_gate, w_ple_up):
    bsz, s_len, _ = x.shape
    depth = w_in.shape[0]
    n_tok = bsz * s_len
    assert s_len % ATT_BLOCK == 0 and s_len % CHUNK == 0 and n_tok % min(TOKEN_TILE, n_tok) == 0
    v_first = None
    for i in range(depth):
        mu = mu_shift[i]
        zero_row = jnp.zeros((WIDTH,), F32)
        pv = jnp.stack([
            mu[:WIDTH], mu[WIDTH:2 * WIDTH], mu[2 * WIDTH:3 * WIDTH], w0[i], a0[i], k_k[i], k_a[i],
            r_k[i].reshape(WIDTH), gn_g[i], gn_b[i], v0[i - 1] if i else zero_row]
            + [zero_row] * (P_ROWS - 11))
        lora0 = 3 * WIDTH
        pm = jnp.pad(mu[lora0:].reshape(2, PAIR), ((0, 6), (0, 0)))
        wd = _pad_rows(w_decay_up[i], 0, PAIR).astype(BF16)
        wa = _pad_rows(w_aaa_up[i], DECAY_LORA, PAIR).astype(BF16)
        wg = w_gate_up[i].astype(BF16)
        wvu = _pad_rows(w_vres_up[i - 1], 0, PAIR).astype(BF16) if i else None

        z = _in_proj(x, g_mix[i][None], _cat_weight(w_in[i], w_vres_down[i - 1] if i else None))
        y_fox = _fox(z, _pad_cols(b_f[i][None], PAIR), jnp.tile(g_qnorm[i], 2)[None], jnp.tile(g_knorm[i], 2)[None])
        y_rwkv, v_first = _rwkv(z, v_first, pv, pm, wd, wa, wg, wvu)
        x2 = _merge(x.reshape(n_tok, D_MODEL), y_fox.reshape(n_tok, WIDTH), y_rwkv.reshape(n_tok, WIDTH),
                    z.reshape(n_tok, N_CAT), w_o_fox[i].astype(BF16), w_o_rwkv[i].astype(BF16),
                    w_out[i].astype(BF16))
        x = _ffn(x2.reshape(bsz, s_len, D_MODEL), g_ffn[i][None], w_up[i].astype(BF16), conv_w[i],
                 conv_b[i][None], w_down[i].astype(BF16))
        x2 = _ple(x.reshape(n_tok, D_MODEL), p[i].reshape(n_tok, PLE_DIM), g_ple[i][None],
                  w_ple_gate[i].astype(BF16), w_ple_up[i].astype(BF16))
        x = x2.reshape(bsz, s_len, D_MODEL)
    return x
```

```python
import functools
import math

import jax
import jax.numpy as jnp
from jax import lax
from jax.experimental import pallas as pl
from jax.experimental.pallas import tpu as pltpu

F32 = jnp.float32
BF16 = jnp.bfloat16

LANES = 128
HEAD_DIM = 64
HEADS = 8
PAIR = 2 * HEAD_DIM
N_PAIRS = HEADS // 2
WIDTH = HEADS * HEAD_DIM
D_MODEL = 1024
D_FF = 2816
PLE_DIM = 256
DECAY_LORA = 64
AAA_LORA = 64
GATE_LORA = 128
VRES_LORA = 32
RMS_EPS = 1e-6
GN_EPS = 64e-5
NEG_BIG = -1e30

COL_GATE_FOX = 0
COL_GATE_RWKV = 8
COL_Q = 16
COL_K = 20
COL_V = 24
COL_R = 28
COL_KR = 32
COL_VR = 36
COL_DWDA = 40
COL_DG = 41
COL_F = 42
COL_VRES = 43
N_CAT = 44 * LANES
F_POS = HEAD_DIM
F_NEG = HEAD_DIM + 32

IN_TILE = 512
TOKEN_TILE = 512
FF_TILE = 256
FF_ROWS = 512
SUBLANES = 8
DECAY_SCALE = math.exp(-0.5)
LOG2_E = math.log2(math.e)
GELU_C0 = 0.7978845608028654
GELU_C1 = 0.7978845608028654 * 0.044715
CHUNK = 64
CHUNK_GROUP = 8
ATT_BLOCK = 256
VMEM_LIMIT = 56 * 1024 * 1024

(P_MU_R, P_MU_K, P_MU_V, P_W0, P_A0, P_KK, P_KA, P_RK, P_GN_G, P_GN_B, P_V0) = range(11)
P_ROWS = 16


def _mm(a, b):
    return jnp.dot(a.astype(BF16), b.astype(BF16), preferred_element_type=F32)


def _mm_nt(a, b):
    return lax.dot_general(a.astype(BF16), b.astype(BF16), (((1,), (1,)), ((), ())),
                           preferred_element_type=F32)


def _mm_tn(a, b):
    return lax.dot_general(a.astype(BF16), b.astype(BF16), (((0,), (0,)), ((), ())),
                           preferred_element_type=F32)


def _sigmoid(x):
    return 1.0 / (1.0 + jnp.exp(-x))


def _softplus(x):
    return jnp.maximum(x, 0.0) + jnp.log(1.0 + jnp.exp(-jnp.abs(x)))


def _rms_rows(x, g):
    ms = jnp.mean(x * x, axis=-1, keepdims=True)
    return x * lax.rsqrt(ms + RMS_EPS) * g


def _shift_rows(u, k):
    row = lax.broadcasted_iota(jnp.int32, u.shape, 0)
    return jnp.where(row >= k, pltpu.roll(u, k, axis=0), 0.0)


def _cumsum_rows(x, seg):
    pos = lax.broadcasted_iota(jnp.int32, x.shape, 0)
    if seg != x.shape[0]:
        assert seg & (seg - 1) == 0
        pos = pos & (seg - 1)
    k = 1
    while k < seg:
        x = x + jnp.where(pos >= k, pltpu.roll(x, k, axis=0), 0.0)
        k *= 2
    return x


def _pair_sum(x):
    low = lax.broadcasted_iota(jnp.int32, x.shape, 1) < HEAD_DIM
    s0 = jnp.sum(jnp.where(low, x, 0.0), axis=-1, keepdims=True)
    s1 = jnp.sum(jnp.where(low, 0.0, x), axis=-1, keepdims=True)
    return jnp.where(low, s0, s1)


def _in_proj_kernel(x_ref, g_ref, w_ref, fp_ref, z_ref, h_ref):
    @pl.when(pl.program_id(1) == 0)
    def _():
        h_ref[...] = _rms_rows(x_ref[...], g_ref[...]).astype(BF16)

    z_ref[...] = jnp.dot(h_ref[...], w_ref[...], preferred_element_type=F32)

    @pl.when(pl.program_id(1) == COL_F * LANES // IN_TILE)
    def _():
        f0 = COL_F * LANES % IN_TILE
        log_f = -_softplus(-(z_ref[:, f0:f0 + LANES] + fp_ref[0:1, :]))
        hi, mid, lo = _split3(_cumsum_rows(log_f, log_f.shape[0]) * fp_ref[3:4, :])
        z_ref[:, f0:f0 + LANES] = jnp.where(fp_ref[1:2, :] > 0.5, hi, jnp.where(fp_ref[2:3, :] > 0.5, mid, lo))


def _in_proj(x, g, w_cat, f_prm):
    bsz, s_len, _ = x.shape
    return pl.pallas_call(
        _in_proj_kernel,
        grid=(bsz, N_CAT // IN_TILE),
        in_specs=[
            pl.BlockSpec((None, s_len, D_MODEL), lambda b, j: (b, 0, 0)),
            pl.BlockSpec((1, D_MODEL), lambda b, j: (0, 0)),
            pl.BlockSpec((D_MODEL, IN_TILE), lambda b, j: (0, j)),
            pl.BlockSpec((SUBLANES, LANES), lambda b, j: (0, 0)),
        ],
        out_specs=pl.BlockSpec((None, s_len, IN_TILE), lambda b, j: (b, 0, j)),
        out_shape=jax.ShapeDtypeStruct((bsz, s_len, N_CAT), F32),
        scratch_shapes=[pltpu.VMEM((s_len, D_MODEL), BF16)],
        compiler_params=pltpu.CompilerParams(
            dimension_semantics=("arbitrary", "arbitrary"), vmem_limit_bytes=VMEM_LIMIT),
        name="in_proj",
    )(x, g, w_cat, f_prm)


def _split3(c):
    hi = c.astype(BF16).astype(F32)
    r1 = c - hi
    mid = r1.astype(BF16).astype(F32)
    lo = (r1 - mid).astype(BF16).astype(F32)
    return hi, mid, lo


def _fox_kernel(zq_ref, zk_ref, zv_ref, zf_ref, gq_ref, gk_ref, y_ref, q_s, k_s, v_s):
    s_len = zq_ref.shape[0]
    n_blocks = s_len // ATT_BLOCK
    pair = pl.program_id(1)
    lane = lax.broadcasted_iota(jnp.int32, (s_len, PAIR), 1)
    low = lane < HEAD_DIM

    def head_norm(x, g):
        ms = _pair_sum(x * x) * (1.0 / HEAD_DIM)
        return x * lax.rsqrt(ms + RMS_EPS) * g

    q = head_norm(zq_ref[...], gq_ref[...]) * (HEAD_DIM ** -0.5 * LOG2_E)
    k = head_norm(zk_ref[...], gk_ref[...])
    v = zv_ref[...]
    c_parts = zf_ref[...]
    lane_row = lax.broadcasted_iota(jnp.int32, (1, PAIR), 1)

    for hh in range(2):
        head = 2 * pair + hh
        pos = (lane_row >= F_POS + 3 * head) & (lane_row < F_POS + 3 * head + 3)
        neg = (lane_row >= F_NEG + 3 * head) & (lane_row < F_NEG + 3 * head + 3)
        qh = q if hh == 0 else pltpu.roll(q, HEAD_DIM, axis=1)
        kh = k if hh == 0 else pltpu.roll(k, HEAD_DIM, axis=1)
        vh = v if hh == 0 else pltpu.roll(v, HEAD_DIM, axis=1)
        q_s[hh] = jnp.where(low, qh, jnp.where(pos, c_parts, jnp.where(neg, 1.0, 0.0))).astype(BF16)
        k_s[hh] = jnp.where(low, kh, jnp.where(neg, c_parts, jnp.where(pos, 1.0, 0.0))).astype(BF16)
        v_s[hh] = vh.astype(BF16)

    tri = (lax.broadcasted_iota(jnp.int32, (ATT_BLOCK, ATT_BLOCK), 0)
           >= lax.broadcasted_iota(jnp.int32, (ATT_BLOCK, ATT_BLOCK), 1))
    low_blk = lax.broadcasted_iota(jnp.int32, (ATT_BLOCK, PAIR), 1) < HEAD_DIM

    def attend(hh, i):
        q0 = i * ATT_BLOCK
        qb = q_s[hh, q0:q0 + ATT_BLOCK, :]
        s_diag = jnp.where(tri, _mm_nt(qb, k_s[hh, q0:q0 + ATT_BLOCK, :]), NEG_BIG)
        m = jnp.max(s_diag, axis=-1, keepdims=True)
        if i:
            s_past = _mm_nt(qb, k_s[hh, 0:q0, :])
            m = jnp.maximum(m, jnp.max(s_past, axis=-1, keepdims=True))
        p_diag = jnp.exp2(s_diag - m)
        l = jnp.sum(p_diag, axis=-1, keepdims=True)
        acc = _mm(p_diag, v_s[hh, q0:q0 + ATT_BLOCK, :])
        if i:
            p_past = jnp.exp2(s_past - m)
            l = l + jnp.sum(p_past, axis=-1, keepdims=True)
            acc = acc + _mm(p_past, v_s[hh, 0:q0, :])
        return acc / l

    for i in range(n_blocks):
        q0 = i * ATT_BLOCK
        y_ref[q0:q0 + ATT_BLOCK, :] = jnp.where(
            low_blk, attend(0, i), pltpu.roll(attend(1, i), HEAD_DIM, axis=1))


def _fox(z, g_q, g_k):
    bsz, s_len, _ = z.shape

    def col(c0):
        return pl.BlockSpec((None, s_len, PAIR), lambda b, h, c0=c0: (b, 0, c0 + h))

    vec = pl.BlockSpec((1, PAIR), lambda b, h: (0, 0))
    return pl.pallas_call(
        _fox_kernel,
        grid=(bsz, N_PAIRS),
        in_specs=[col(COL_Q), col(COL_K), col(COL_V),
                  pl.BlockSpec((None, s_len, PAIR), lambda b, h: (b, 0, COL_F)), vec, vec],
        out_specs=pl.BlockSpec((None, s_len, PAIR), lambda b, h: (b, 0, h)),
        out_shape=jax.ShapeDtypeStruct((bsz, s_len, WIDTH), F32),
        scratch_shapes=[pltpu.VMEM((2, s_len, PAIR), BF16), pltpu.VMEM((2, s_len, PAIR), BF16),
                        pltpu.VMEM((2, s_len, PAIR), BF16)],
        compiler_params=pltpu.CompilerParams(
            dimension_semantics=("arbitrary", "arbitrary"), vmem_limit_bytes=VMEM_LIMIT),
        name="fox_attention",
    )(z, z, z, z, g_q, g_k)


def _rwkv_kernel(first_layer, *refs):
    if first_layer:
        (zr_ref, zk_ref, zv_ref, zdwda_ref, zdg_ref, pv_ref, pm_ref, wd_ref, wa_ref, wg_ref,
         y_ref, vfirst_out_ref, *scratch) = refs
    else:
        (zr_ref, zk_ref, zv_ref, zdwda_ref, zdg_ref, zvres_ref, vfirst_ref, pv_ref, pm_ref, wd_ref,
         wa_ref, wg_ref, wvu_ref, y_ref, *scratch) = refs
    (ar_s, asm_s, bkm_s, vsm_s, v_s, bhkh_s, gcol_s, g_s, h_s, q_s, y0_s, yraw_s, bonus_s, gate_s) = scratch

    s_len = zr_ref.shape[0]
    c_len = CHUNK
    n_chunks = s_len // c_len
    lane = lax.broadcasted_iota(jnp.int32, (s_len, PAIR), 1)
    low = lane < HEAD_DIM

    def prm(row):
        return pv_ref[row:row + 1, :]

    def mix(u_ref, mu):
        u = u_ref[...]
        prev = jnp.concatenate(
            [_shift_rows(u_ref[:SUBLANES, :], 1), u_ref[SUBLANES - 1:s_len - 1, :]], axis=0)
        return u + mu * (prev - u)

    r = mix(zr_ref, prm(P_MU_R))
    kr = mix(zk_ref, prm(P_MU_K))
    vr = mix(zv_ref, prm(P_MU_V))
    dwda = mix(zdwda_ref, pm_ref[0:1, :])
    dg = mix(zdg_ref, pm_ref[1:2, :])

    log_decay = -DECAY_SCALE * _sigmoid(prm(P_W0) + _mm(jnp.tanh(dwda), wd_ref[...]))
    a_gate = _sigmoid(prm(P_A0) + _mm(dwda, wa_ref[...]))
    gate_s[...] = _mm(_sigmoid(dg), wg_ref[...])
    kk = kr * prm(P_KK)
    kk = kk / jnp.maximum(jnp.sqrt(_pair_sum(kk * kk)), 1e-12)
    kr = kr * (1.0 + (a_gate - 1.0) * prm(P_KA))
    if first_layer:
        vfirst_out_ref[...] = vr
    else:
        v_mix = _sigmoid(prm(P_V0) + _mm(zvres_ref[...], wvu_ref[...]))
        vr = vr + (vfirst_ref[...] - vr) * v_mix
    bonus_s[...] = _pair_sum(r * kr * prm(P_RK)) * vr

    def chunked(u):
        return u.reshape(n_chunks, c_len, PAIR)

    cum = chunked(_cumsum_rows(log_decay, c_len))
    total = cum[:, c_len - 1:c_len, :]
    low3 = lax.broadcasted_iota(jnp.int32, (n_chunks, c_len, PAIR), 2) < HEAD_DIM
    b_vec = kk * a_gate
    r_t = chunked(r) * jnp.exp(cum)
    a_t = chunked(-kk) * jnp.exp(cum - chunked(log_decay))
    inv = jnp.exp(-cum)
    b_t = chunked(b_vec) * inv
    k_t = chunked(kr) * inv
    tail = jnp.exp(total - cum)
    v3 = chunked(vr)

    def masked_stack(u):
        return jnp.concatenate([jnp.where(low3, u, 0.0), jnp.where(low3, 0.0, u)], axis=1)

    ar_s[...] = jnp.concatenate([a_t, r_t], axis=1).astype(BF16)
    asm_s[...] = masked_stack(a_t).astype(BF16)
    bkm_s[...] = jnp.concatenate([masked_stack(b_t), masked_stack(k_t)], axis=1).astype(BF16)
    vsm_s[...] = masked_stack(v3).astype(BF16)
    v_s[...] = v3.astype(BF16)
    bhkh_s[...] = jnp.concatenate([chunked(b_vec) * tail, chunked(kr) * tail], axis=1).astype(BF16)
    eye = (lax.broadcasted_iota(jnp.int32, (PAIR, PAIR), 0) == lax.broadcasted_iota(jnp.int32, (PAIR, PAIR), 1))
    eye3 = (lax.broadcasted_iota(jnp.int32, (n_chunks, PAIR, PAIR), 1)
            == lax.broadcasted_iota(jnp.int32, (n_chunks, PAIR, PAIR), 2))
    g_c = jnp.exp(total)
    g_diag = jnp.where(eye3, jnp.broadcast_to(g_c, (n_chunks, PAIR, PAIR)), 0.0)
    gcol_s[...] = jnp.broadcast_to(jnp.sum(g_diag, axis=-1, keepdims=True), (n_chunks, PAIR, PAIR))

    row_c = lax.broadcasted_iota(jnp.int32, (c_len, 2 * c_len), 0)
    col_c = lax.broadcasted_iota(jnp.int32, (c_len, 2 * c_len), 1)
    strict0 = (col_c < c_len) & (row_c > col_c)
    strict1 = (col_c >= c_len) & (row_c > col_c - c_len)
    row_w = lax.broadcasted_iota(jnp.int32, (c_len, 4 * c_len), 0)
    col_w = lax.broadcasted_iota(jnp.int32, (c_len, 4 * c_len), 1)
    lower_w = row_w >= (col_w & (c_len - 1))
    same_head = ((lax.broadcasted_iota(jnp.int32, (PAIR, PAIR), 0) < HEAD_DIM)
                 == (lax.broadcasted_iota(jnp.int32, (PAIR, PAIR), 1) < HEAD_DIM))
    eye_f = jnp.where(eye, 1.0, 0.0)
    zeros_cp = jnp.zeros((c_len, PAIR), BF16)
    zeros_2cp = jnp.zeros((2 * c_len, PAIR), BF16)

    def block_diag(x):
        return jnp.concatenate([jnp.where(strict0, x, 0.0), jnp.where(strict1, x, 0.0)], axis=0)

    group = CHUNK_GROUP if n_chunks % CHUNK_GROUP == 0 else 1

    def chunk_terms(cs):
        ars = [ar_s[c] for c in cs]
        ps = [_mm_nt(ar, bkm_s[c]) for ar, c in zip(ars, cs)]
        l_pows = [block_diag(p[:c_len, :2 * c_len]) for p in ps]
        ts = [eye_f + l for l in l_pows]
        yield
        for _ in range(c_len.bit_length() - 2):
            l_pows = [_mm(l, l) for l in l_pows]
            ts = [t + _mm(t, l) for t, l in zip(ts, l_pows)]
            yield
        vsms = [vsm_s[c] for c in cs]
        lvs = [_mm(block_diag(p[:c_len, 2 * c_len:]), vsm) for p, vsm in zip(ps, vsms)]
        yield
        zs = [_mm(t, jnp.concatenate([asm_s[c], lv.astype(BF16)], axis=1))
              for t, c, lv in zip(ts, cs, lvs)]
        yield
        for c, ar, p, vsm, z in zip(cs, ars, ps, vsms, zs):
            zz = (z[:c_len] + z[c_len:]).astype(BF16)
            rhs4 = jnp.concatenate([zz, jnp.concatenate([zeros_cp, v_s[c]], axis=1)], axis=0)
            gh = _mm_tn(bhkh_s[c], rhs4)
            g_s[c] = jnp.where(same_head, gh[:, :PAIR], 0.0).astype(BF16)
            h_s[c] = jnp.where(same_head, gh[:, PAIR:], 0.0)
            rhs5 = jnp.concatenate(
                [z.astype(BF16), jnp.concatenate([zeros_2cp, vsm], axis=1)], axis=0)
            qy = _mm(jnp.where(lower_w, p[c_len:], 0.0), rhs5)
            q_s[c] = (ar[c_len:].astype(F32) + qy[:, :PAIR]).astype(BF16)
            y0_s[c] = qy[:, PAIR:]

    def recur(c, m):
        mb = m.astype(BF16)
        yraw_s[c] = _mm(q_s[c], mb) + y0_s[c]
        return gcol_s[c] * m + _mm(g_s[c], mb) + h_s[c]

    m = jnp.zeros((PAIR, PAIR), F32)
    pending = []
    for cg in range(n_chunks // group):
        cs = list(range(cg * group, (cg + 1) * group))
        for _ in chunk_terms(cs):
            if pending:
                m = recur(pending.pop(0), m)
        while pending:
            m = recur(pending.pop(0), m)
        pending = cs
    for c in pending:
        m = recur(c, m)

    y = yraw_s[...].reshape(s_len, PAIR)
    mu = _pair_sum(y) * (1.0 / HEAD_DIM)
    d = y - mu
    var = _pair_sum(d * d) * (1.0 / HEAD_DIM)
    yn = d * lax.rsqrt(var + GN_EPS) * prm(P_GN_G) + prm(P_GN_B)
    y_ref[...] = (yn + bonus_s[...]) * gate_s[...]


def _rwkv(z, v_first, pv, pm, wd, wa, wg, wvu):
    bsz, s_len, _ = z.shape
    first_layer = v_first is None
    n_chunks = s_len // CHUNK

    def col(c0):
        return pl.BlockSpec((None, s_len, PAIR), lambda b, h, c0=c0: (b, 0, c0 + h))

    def fixed(c0):
        return pl.BlockSpec((None, s_len, PAIR), lambda b, h, c0=c0: (b, 0, c0))

    pair_cols = pl.BlockSpec((None, s_len, PAIR), lambda b, h: (b, 0, h))
    w_spec = pl.BlockSpec((PAIR, PAIR), lambda b, h: (0, h))
    in_specs = [col(COL_R), col(COL_KR), col(COL_VR), fixed(COL_DWDA), fixed(COL_DG)]
    args = [z, z, z, z, z]
    if not first_layer:
        in_specs += [fixed(COL_VRES), pair_cols]
        args += [z, v_first]
    in_specs += [pl.BlockSpec((P_ROWS, PAIR), lambda b, h: (0, h)),
                 pl.BlockSpec((8, PAIR), lambda b, h: (0, 0)), w_spec, w_spec, w_spec]
    args += [pv, pm, wd, wa, wg]
    out_shape = [jax.ShapeDtypeStruct((bsz, s_len, WIDTH), F32)]
    out_specs = [pair_cols]
    if first_layer:
        out_shape.append(jax.ShapeDtypeStruct((bsz, s_len, WIDTH), F32))
        out_specs.append(pair_cols)
    else:
        in_specs.append(w_spec)
        args.append(wvu)
    c2, c4 = 2 * CHUNK, 4 * CHUNK
    scratch = [
        pltpu.VMEM((n_chunks, c2, PAIR), BF16),
        pltpu.VMEM((n_chunks, c2, PAIR), BF16),
        pltpu.VMEM((n_chunks, c4, PAIR), BF16),
        pltpu.VMEM((n_chunks, c2, PAIR), BF16),
        pltpu.VMEM((n_chunks, CHUNK, PAIR), BF16),
        pltpu.VMEM((n_chunks, c2, PAIR), BF16),
        pltpu.VMEM((n_chunks, PAIR, PAIR), F32),
        pltpu.VMEM((n_chunks, PAIR, PAIR), BF16),
        pltpu.VMEM((n_chunks, PAIR, PAIR), F32),
        pltpu.VMEM((n_chunks, CHUNK, PAIR), BF16),
        pltpu.VMEM((n_chunks, CHUNK, PAIR), F32),
        pltpu.VMEM((n_chunks, CHUNK, PAIR), F32),
        pltpu.VMEM((s_len, PAIR), F32),
        pltpu.VMEM((s_len, PAIR), F32),
    ]
    out = pl.pallas_call(
        functools.partial(_rwkv_kernel, first_layer),
        grid=(bsz, N_PAIRS),
        in_specs=in_specs,
        out_specs=out_specs,
        out_shape=out_shape,
        scratch_shapes=scratch,
        compiler_params=pltpu.CompilerParams(
            dimension_semantics=("arbitrary", "arbitrary"), vmem_limit_bytes=VMEM_LIMIT),
        name="rwkv_first" if first_layer else "rwkv",
    )(*args)
    return (out[0], out[1]) if first_layer else (out[0], v_first)


def _merge_kernel(x_ref, yf_ref, yr_ref, gf_ref, gr_ref, wf_ref, wr_ref, wo_ref, o_ref):
    merged = (_sigmoid(gf_ref[...]) * _mm(yf_ref[...], wf_ref[...])
              + _sigmoid(gr_ref[...]) * _mm(yr_ref[...], wr_ref[...]))
    o_ref[...] = x_ref[...] + _mm(merged, wo_ref[...])


def _merge(x2, yf2, yr2, z2, w_of, w_or, w_out):
    n_tok = x2.shape[0]
    tile = min(TOKEN_TILE, n_tok)
    gate_blocks = D_MODEL // LANES

    def rows(width, c0=0):
        return pl.BlockSpec((tile, width), lambda i, c0=c0: (i, c0))

    def whole(a):
        return pl.BlockSpec(a.shape, lambda i: (0, 0))

    return pl.pallas_call(
        _merge_kernel,
        grid=(n_tok // tile,),
        in_specs=[rows(D_MODEL), rows(WIDTH), rows(WIDTH), rows(D_MODEL, COL_GATE_FOX // gate_blocks),
                  rows(D_MODEL, COL_GATE_RWKV // gate_blocks), whole(w_of), whole(w_or), whole(w_out)],
        out_specs=rows(D_MODEL),
        out_shape=jax.ShapeDtypeStruct(x2.shape, F32),
        compiler_params=pltpu.CompilerParams(
            dimension_semantics=("arbitrary",), vmem_limit_bytes=VMEM_LIMIT),
        name="merge_out",
    )(x2, yf2, yr2, z2, z2, w_of, w_or, w_out)


def _ffn_kernel(x_ref, g_ref, wu_ref, cw_ref, cb_ref, wd_ref, o_ref, h_s, act_s, tail_s):
    rb = x_ref.shape[0]
    n_ff = D_FF // FF_TILE
    first = pl.program_id(1) == 0
    x = x_ref[...]
    h_s[...] = _rms_rows(x, g_ref[...]).astype(BF16)

    def conv(u, tail, c0):
        full = jnp.concatenate([tail, u], axis=0)
        return (cb_ref[:, c0:c0 + FF_TILE] + full[SUBLANES - 2:SUBLANES - 2 + rb] * cw_ref[0:1, c0:c0 + FF_TILE]
                + full[SUBLANES - 1:SUBLANES - 1 + rb] * cw_ref[1:2, c0:c0 + FF_TILE]
                + u * cw_ref[2:3, c0:c0 + FF_TILE])

    def down(j0, j1):
        return jnp.dot(act_s[:, j0 * FF_TILE:j1 * FF_TILE], wd_ref[j0 * FF_TILE:j1 * FF_TILE, :],
                       preferred_element_type=F32)

    half = (n_ff + 1) // 2
    out = x
    for j in range(n_ff):
        cs = []
        for part in range(2):
            c0 = part * D_FF + j * FF_TILE
            u = jnp.dot(h_s[...], wu_ref[:, c0:c0 + FF_TILE], preferred_element_type=F32)
            tail = jnp.where(first, 0.0, tail_s[part, j])
            tail_s[part, j] = u[rb - SUBLANES:]
            cs.append(conv(u, tail, c0))
        c1, c2 = cs
        gelu = 0.5 * c1 * (1.0 + jnp.tanh(c1 * (GELU_C0 + GELU_C1 * (c1 * c1))))
        act_s[:, j * FF_TILE:(j + 1) * FF_TILE] = (gelu * c2).astype(BF16)
        if j == half - 1:
            out = out + down(0, half)
    o_ref[...] = out + down(half, n_ff)


def _ffn(x, g, w_up, conv_w, conv_b, w_down):
    bsz, s_len, _ = x.shape
    n_ff = D_FF // FF_TILE
    rb = min(FF_ROWS, s_len)
    assert s_len % rb == 0
    row_spec = pl.BlockSpec((None, rb, D_MODEL), lambda b, r: (b, r, 0))

    def whole(a):
        return pl.BlockSpec(a.shape, lambda b, r: (0, 0))

    return pl.pallas_call(
        _ffn_kernel,
        grid=(bsz, s_len // rb),
        in_specs=[row_spec, whole(g), whole(w_up), whole(conv_w), whole(conv_b), whole(w_down)],
        out_specs=row_spec,
        out_shape=jax.ShapeDtypeStruct(x.shape, F32),
        scratch_shapes=[pltpu.VMEM((rb, D_MODEL), BF16), pltpu.VMEM((rb, D_FF), BF16),
                        pltpu.VMEM((2, n_ff, SUBLANES, FF_TILE), F32)],
        compiler_params=pltpu.CompilerParams(
            dimension_semantics=("arbitrary", "arbitrary"), vmem_limit_bytes=VMEM_LIMIT),
        name="conv_ffn",
    )(x, g, w_up, conv_w, conv_b, w_down)


def _ple_kernel(x_ref, p_ref, g_ref, wg_ref, wu_ref, o_ref):
    x = x_ref[...]
    gate = _sigmoid(_mm(_rms_rows(x, g_ref[...]), wg_ref[...]))
    o_ref[...] = x + gate * _mm(p_ref[...], wu_ref[...])


def _ple(x2, p2, g, w_gate, w_up):
    n_tok = x2.shape[0]
    tile = min(TOKEN_TILE, n_tok)
    return pl.pallas_call(
        _ple_kernel,
        grid=(n_tok // tile,),
        in_specs=[pl.BlockSpec((tile, D_MODEL), lambda i: (i, 0)), pl.BlockSpec((tile, PLE_DIM), lambda i: (i, 0)),
                  pl.BlockSpec((1, D_MODEL), lambda i: (0, 0)), pl.BlockSpec(w_gate.shape, lambda i: (0, 0)),
                  pl.BlockSpec(w_up.shape, lambda i: (0, 0))],
        out_specs=pl.BlockSpec((tile, D_MODEL), lambda i: (i, 0)),
        out_shape=jax.ShapeDtypeStruct(x2.shape, F32),
        compiler_params=pltpu.CompilerParams(
            dimension_semantics=("arbitrary",), vmem_limit_bytes=VMEM_LIMIT),
        name="ple",
    )(x2, p2, g, w_gate, w_up)


def _pad_cols(a, width):
    return jnp.pad(a, ((0, 0), (0, width - a.shape[1])))


def _pad_rows(a, top, total):
    return jnp.pad(a, ((top, total - top - a.shape[0]), (0, 0)))


def _f_lanes(a):
    trip = jnp.repeat(a, 3, axis=1)
    gap = jnp.zeros((a.shape[0], F_NEG - F_POS - 3 * HEADS), a.dtype)
    return _pad_cols(jnp.concatenate([jnp.zeros((a.shape[0], F_POS), a.dtype), trip, gap, trip], axis=1), LANES)


def _cat_weight(w_in, w_vres_down):
    fox = 3 * WIDTH
    rw0 = fox + HEADS
    lora0 = rw0 + 3 * WIDTH
    gate0 = lora0 + DECAY_LORA + AAA_LORA + GATE_LORA
    vres = (jnp.zeros((D_MODEL, LANES), F32) if w_vres_down is None else _pad_cols(w_vres_down, LANES))
    return jnp.concatenate([
        w_in[:, gate0:], w_in[:, :fox], w_in[:, rw0:lora0], w_in[:, lora0:gate0],
        _f_lanes(w_in[:, fox:rw0]), vres], axis=1).astype(BF16)


def kernel(x, p, g_mix, w_in, b_f, g_qnorm, g_knorm, mu_shift, w_decay_up, w0, w_aaa_up, a0, w_gate_up, k_k, k_a, r_k, gn_g, gn_b, w_vres_down, w_vres_up, v0, w_o_fox, w_o_rwkv, w_out, g_ffn, w_up, conv_w, conv_b, w_down, g_ple, w_ple_gate, w_ple_up):
    bsz, s_len, _ = x.shape
    depth = w_in.shape[0]
    n_tok = bsz * s_len
    assert s_len % ATT_BLOCK == 0 and s_len % CHUNK == 0 and n_tok % min(TOKEN_TILE, n_tok) == 0
    v_first = None
    lane_ids = jnp.arange(LANES)
    lane_mod3 = jnp.where(lane_ids < F_NEG, (lane_ids - F_POS) % 3, (lane_ids - F_NEG) % 3)
    for i in range(depth):
        mu = mu_shift[i]
        zero_row = jnp.zeros((WIDTH,), F32)
        pv = jnp.stack([
            mu[:WIDTH], mu[WIDTH:2 * WIDTH], mu[2 * WIDTH:3 * WIDTH], w0[i], a0[i], k_k[i], k_a[i],
            r_k[i].reshape(WIDTH), gn_g[i], gn_b[i], v0[i - 1] if i else zero_row]
            + [zero_row] * (P_ROWS - 11))
        lora0 = 3 * WIDTH
        pm = jnp.pad(mu[lora0:].reshape(2, PAIR), ((0, 6), (0, 0)))
        wd = _pad_rows(w_decay_up[i], 0, PAIR).astype(BF16)
        wa = _pad_rows(w_aaa_up[i], DECAY_LORA, PAIR).astype(BF16)
        wg = w_gate_up[i].astype(BF16)
        wvu = _pad_rows(w_vres_up[i - 1], 0, PAIR).astype(BF16) if i else None

        ones = jnp.ones((1, HEADS), F32)
        f_prm = jnp.pad(jnp.concatenate([
            _f_lanes(b_f[i][None]), _f_lanes(ones) * (lane_mod3 == 0), _f_lanes(ones) * (lane_mod3 == 1),
            _f_lanes(ones) * jnp.where(lane_ids < F_NEG, LOG2_E, -LOG2_E)]), ((0, SUBLANES - 4), (0, 0)))
        z = _in_proj(x, g_mix[i][None], _cat_weight(w_in[i], w_vres_down[i - 1] if i else None), f_prm)
        y_fox = _fox(z, jnp.tile(g_qnorm[i], 2)[None], jnp.tile(g_knorm[i], 2)[None])
        y_rwkv, v_first = _rwkv(z, v_first, pv, pm, wd, wa, wg, wvu)
        x2 = _merge(x.reshape(n_tok, D_MODEL), y_fox.reshape(n_tok, WIDTH), y_rwkv.reshape(n_tok, WIDTH),
                    z.reshape(n_tok, N_CAT), w_o_fox[i].astype(BF16), w_o_rwkv[i].astype(BF16),
                    w_out[i].astype(BF16))
        x = _ffn(x2.reshape(bsz, s_len, D_MODEL), g_ffn[i][None], w_up[i].astype(BF16), conv_w[i],
                 conv_b[i][None], w_down[i].astype(BF16))
        x2 = _ple(x.reshape(n_tok, D_MODEL), p[i].reshape(n_tok, PLE_DIM), g_ple[i][None],
                  w_ple_gate[i].astype(BF16), w_ple_up[i].astype(BF16))
        x = x2.reshape(bsz, s_len, D_MODEL)
    return x
```

```python
import functools
import math

import jax
import jax.numpy as jnp
from jax import lax
from jax.experimental import pallas as pl
from jax.experimental.pallas import tpu as pltpu

F32 = jnp.float32
BF16 = jnp.bfloat16

LANES = 128
HEAD_DIM = 64
HEADS = 8
PAIR = 2 * HEAD_DIM
N_PAIRS = HEADS // 2
WIDTH = HEADS * HEAD_DIM
D_MODEL = 1024
D_FF = 2816
PLE_DIM = 256
DECAY_LORA = 64
AAA_LORA = 64
GATE_LORA = 128
VRES_LORA = 32
RMS_EPS = 1e-6
GN_EPS = 64e-5
NEG_BIG = -1e30

COL_GATE_FOX = 0
COL_GATE_RWKV = 8
COL_Q = 16
COL_K = 20
COL_V = 24
COL_R = 28
COL_KR = 32
COL_VR = 36
COL_DWDA = 40
COL_DG = 41
COL_F = 42
COL_VRES = 43
N_CAT = 44 * LANES
F_POS = HEAD_DIM
F_NEG = HEAD_DIM + 32

IN_TILE = 512
TOKEN_TILE = 512
FF_TILE = 256
FF_ROWS = 512
SUBLANES = 8
DECAY_SCALE = math.exp(-0.5)
LOG2_E = math.log2(math.e)
GELU_C0 = 0.7978845608028654
GELU_C1 = 0.7978845608028654 * 0.044715
CHUNK = 64
CHUNK_GROUP = 8
ATT_BLOCK = 256
VMEM_LIMIT = 56 * 1024 * 1024

(P_MU_R, P_MU_K, P_MU_V, P_W0, P_A0, P_KK, P_KA, P_RK, P_GN_G, P_GN_B, P_V0) = range(11)
P_ROWS = 16


def _mm(a, b):
    return jnp.dot(a.astype(BF16), b.astype(BF16), preferred_element_type=F32)


def _mm_nt(a, b):
    return lax.dot_general(a.astype(BF16), b.astype(BF16), (((1,), (1,)), ((), ())),
                           preferred_element_type=F32)


def _mm_tn(a, b):
    return lax.dot_general(a.astype(BF16), b.astype(BF16), (((0,), (0,)), ((), ())),
                           preferred_element_type=F32)


def _sigmoid(x):
    return 1.0 / (1.0 + jnp.exp(-x))


def _softplus(x):
    return jnp.maximum(x, 0.0) + jnp.log(1.0 + jnp.exp(-jnp.abs(x)))


def _rms_rows(x, g):
    ms = jnp.mean(x * x, axis=-1, keepdims=True)
    return x * lax.rsqrt(ms + RMS_EPS) * g


def _shift_rows(u, k):
    row = lax.broadcasted_iota(jnp.int32, u.shape, 0)
    return jnp.where(row >= k, pltpu.roll(u, k, axis=0), 0.0)


def _cumsum_rows(x, seg):
    pos = lax.broadcasted_iota(jnp.int32, x.shape, 0)
    if seg != x.shape[0]:
        assert seg & (seg - 1) == 0
        pos = pos & (seg - 1)
    k = 1
    while k < seg:
        x = x + jnp.where(pos >= k, pltpu.roll(x, k, axis=0), 0.0)
        k *= 2
    return x


def _pair_sum(x):
    low = lax.broadcasted_iota(jnp.int32, x.shape, 1) < HEAD_DIM
    s0 = jnp.sum(jnp.where(low, x, 0.0), axis=-1, keepdims=True)
    s1 = jnp.sum(jnp.where(low, 0.0, x), axis=-1, keepdims=True)
    return jnp.where(low, s0, s1)


def _in_proj_kernel(x_ref, g_ref, w_ref, fp_ref, z_ref, h_ref):
    @pl.when(pl.program_id(1) == 0)
    def _():
        h_ref[...] = _rms_rows(x_ref[...], g_ref[...]).astype(BF16)

    z_ref[...] = jnp.dot(h_ref[...], w_ref[...], preferred_element_type=F32)

    @pl.when(pl.program_id(1) == COL_F * LANES // IN_TILE)
    def _():
        f0 = COL_F * LANES % IN_TILE
        log_f = -_softplus(-(z_ref[:, f0:f0 + LANES] + fp_ref[0:1, :]))
        hi, mid, lo = _split3(_cumsum_rows(log_f, log_f.shape[0]) * fp_ref[3:4, :])
        z_ref[:, f0:f0 + LANES] = jnp.where(fp_ref[1:2, :] > 0.5, hi, jnp.where(fp_ref[2:3, :] > 0.5, mid, lo))


def _in_proj(x, g, w_cat, f_prm):
    bsz, s_len, _ = x.shape
    return pl.pallas_call(
        _in_proj_kernel,
        grid=(bsz, N_CAT // IN_TILE),
        in_specs=[
            pl.BlockSpec((None, s_len, D_MODEL), lambda b, j: (b, 0, 0)),
            pl.BlockSpec((1, D_MODEL), lambda b, j: (0, 0)),
            pl.BlockSpec((D_MODEL, IN_TILE), lambda b, j: (0, j)),
            pl.BlockSpec((SUBLANES, LANES), lambda b, j: (0, 0)),
        ],
        out_specs=pl.BlockSpec((None, s_len, IN_TILE), lambda b, j: (b, 0, j)),
        out_shape=jax.ShapeDtypeStruct((bsz, s_len, N_CAT), F32),
        scratch_shapes=[pltpu.VMEM((s_len, D_MODEL), BF16)],
        compiler_params=pltpu.CompilerParams(
            dimension_semantics=("arbitrary", "arbitrary"), vmem_limit_bytes=VMEM_LIMIT),
        name="in_proj",
    )(x, g, w_cat, f_prm)


def _split3(c):
    hi = c.astype(BF16).astype(F32)
    r1 = c - hi
    mid = r1.astype(BF16).astype(F32)
    lo = (r1 - mid).astype(BF16).astype(F32)
    return hi, mid, lo


def _fox_kernel(zq_ref, zk_ref, zv_ref, zf_ref, gq_ref, gk_ref, y_ref, q_s, k_s, v_s):
    s_len = zq_ref.shape[0]
    n_blocks = s_len // ATT_BLOCK
    pair = pl.program_id(1)
    lane = lax.broadcasted_iota(jnp.int32, (s_len, PAIR), 1)
    low = lane < HEAD_DIM

    def head_norm(x, g):
        ms = _pair_sum(x * x) * (1.0 / HEAD_DIM)
        return x * lax.rsqrt(ms + RMS_EPS) * g

    q = head_norm(zq_ref[...], gq_ref[...]) * (HEAD_DIM ** -0.5 * LOG2_E)
    k = head_norm(zk_ref[...], gk_ref[...])
    v = zv_ref[...]
    c_parts = zf_ref[...]
    lane_row = lax.broadcasted_iota(jnp.int32, (1, PAIR), 1)

    for hh in range(2):
        head = 2 * pair + hh
        pos = (lane_row >= F_POS + 3 * head) & (lane_row < F_POS + 3 * head + 3)
        neg = (lane_row >= F_NEG + 3 * head) & (lane_row < F_NEG + 3 * head + 3)
        qh = q if hh == 0 else pltpu.roll(q, HEAD_DIM, axis=1)
        kh = k if hh == 0 else pltpu.roll(k, HEAD_DIM, axis=1)
        vh = v if hh == 0 else pltpu.roll(v, HEAD_DIM, axis=1)
        q_s[hh] = jnp.where(low, qh, jnp.where(pos, c_parts, jnp.where(neg, 1.0, 0.0))).astype(BF16)
        k_s[hh] = jnp.where(low, kh, jnp.where(neg, c_parts, jnp.where(pos, 1.0, 0.0))).astype(BF16)
        v_s[hh] = vh.astype(BF16)

    tri = (lax.broadcasted_iota(jnp.int32, (ATT_BLOCK, ATT_BLOCK), 0)
           >= lax.broadcasted_iota(jnp.int32, (ATT_BLOCK, ATT_BLOCK), 1))
    low_blk = lax.broadcasted_iota(jnp.int32, (ATT_BLOCK, PAIR), 1) < HEAD_DIM

    def attend(hh, i):
        q0 = i * ATT_BLOCK
        qb = q_s[hh, q0:q0 + ATT_BLOCK, :]
        s_diag = jnp.where(tri, _mm_nt(qb, k_s[hh, q0:q0 + ATT_BLOCK, :]), NEG_BIG)
        m = jnp.max(s_diag, axis=-1, keepdims=True)
        if i:
            s_past = _mm_nt(qb, k_s[hh, 0:q0, :])
            m = jnp.maximum(m, jnp.max(s_past, axis=-1, keepdims=True))
        p_diag = jnp.exp2(s_diag - m)
        l = jnp.sum(p_diag, axis=-1, keepdims=True)
        acc = _mm(p_diag, v_s[hh, q0:q0 + ATT_BLOCK, :])
        if i:
            p_past = jnp.exp2(s_past - m)
            l = l + jnp.sum(p_past, axis=-1, keepdims=True)
            acc = acc + _mm(p_past, v_s[hh, 0:q0, :])
        return acc / l

    for i in range(n_blocks):
        q0 = i * ATT_BLOCK
        y_ref[q0:q0 + ATT_BLOCK, :] = jnp.where(
            low_blk, attend(0, i), pltpu.roll(attend(1, i), HEAD_DIM, axis=1))


def _fox(z, g_q, g_k):
    bsz, s_len, _ = z.shape

    def col(c0):
        return pl.BlockSpec((None, s_len, PAIR), lambda b, h, c0=c0: (b, 0, c0 + h))

    vec = pl.BlockSpec((1, PAIR), lambda b, h: (0, 0))
    return pl.pallas_call(
        _fox_kernel,
        grid=(bsz, N_PAIRS),
        in_specs=[col(COL_Q), col(COL_K), col(COL_V),
                  pl.BlockSpec((None, s_len, PAIR), lambda b, h: (b, 0, COL_F)), vec, vec],
        out_specs=pl.BlockSpec((None, s_len, PAIR), lambda b, h: (b, 0, h)),
        out_shape=jax.ShapeDtypeStruct((bsz, s_len, WIDTH), F32),
        scratch_shapes=[pltpu.VMEM((2, s_len, PAIR), BF16), pltpu.VMEM((2, s_len, PAIR), BF16),
                        pltpu.VMEM((2, s_len, PAIR), BF16)],
        compiler_params=pltpu.CompilerParams(
            dimension_semantics=("arbitrary", "arbitrary"), vmem_limit_bytes=VMEM_LIMIT),
        name="fox_attention",
    )(z, z, z, z, g_q, g_k)


def _rwkv_kernel(first_layer, *refs):
    if first_layer:
        (zr_ref, zk_ref, zv_ref, zdwda_ref, zdg_ref, pv_ref, pm_ref, wd_ref, wa_ref, wg_ref,
         y_ref, vfirst_out_ref, *scratch) = refs
    else:
        (zr_ref, zk_ref, zv_ref, zdwda_ref, zdg_ref, zvres_ref, vfirst_ref, pv_ref, pm_ref, wd_ref,
         wa_ref, wg_ref, wvu_ref, y_ref, *scratch) = refs
    (ar_s, asm_s, bkm_s, vsm_s, v_s, bhkh_s, gcol_s, g_s, h_s, q_s, y0_s, yraw_s, bonus_s, gate_s) = scratch

    s_len = zr_ref.shape[0]
    c_len = CHUNK
    n_chunks = s_len // c_len
    eye = (lax.broadcasted_iota(jnp.int32, (PAIR, PAIR), 0) == lax.broadcasted_iota(jnp.int32, (PAIR, PAIR), 1))

    def prm(row):
        return pv_ref[row:row + 1, :]

    def prepare(c0, nc):
        r0, n = c0 * c_len, nc * c_len
        rows = slice(r0, r0 + n)

        def mix(u_ref, mu):
            u = u_ref[rows, :]
            if r0:
                prev = u_ref[r0 - 1:r0 + n - 1, :]
            else:
                prev = jnp.concatenate(
                    [_shift_rows(u_ref[:SUBLANES, :], 1), u_ref[SUBLANES - 1:n - 1, :]], axis=0)
            return u + mu * (prev - u)

        r = mix(zr_ref, prm(P_MU_R))
        kr = mix(zk_ref, prm(P_MU_K))
        vr = mix(zv_ref, prm(P_MU_V))
        dwda = mix(zdwda_ref, pm_ref[0:1, :])
        dg = mix(zdg_ref, pm_ref[1:2, :])

        log_decay = -DECAY_SCALE * _sigmoid(prm(P_W0) + _mm(jnp.tanh(dwda), wd_ref[...]))
        a_gate = _sigmoid(prm(P_A0) + _mm(dwda, wa_ref[...]))
        gate_s[rows, :] = _mm(_sigmoid(dg), wg_ref[...])
        kk = kr * prm(P_KK)
        kk = kk / jnp.maximum(jnp.sqrt(_pair_sum(kk * kk)), 1e-12)
        kr = kr * (1.0 + (a_gate - 1.0) * prm(P_KA))
        if first_layer:
            vfirst_out_ref[rows, :] = vr
        else:
            v_mix = _sigmoid(prm(P_V0) + _mm(zvres_ref[rows, :], wvu_ref[...]))
            vr = vr + (vfirst_ref[rows, :] - vr) * v_mix
        bonus_s[rows, :] = _pair_sum(r * kr * prm(P_RK)) * vr

        def chunked(u):
            return u.reshape(nc, c_len, PAIR)

        cum = chunked(_cumsum_rows(log_decay, c_len))
        total = cum[:, c_len - 1:c_len, :]
        low3 = lax.broadcasted_iota(jnp.int32, (nc, c_len, PAIR), 2) < HEAD_DIM
        b_vec = kk * a_gate
        r_t = chunked(r) * jnp.exp(cum)
        a_t = chunked(-kk) * jnp.exp(cum - chunked(log_decay))
        inv = jnp.exp(-cum)
        b_t = chunked(b_vec) * inv
        k_t = chunked(kr) * inv
        tail = jnp.exp(total - cum)
        v3 = chunked(vr)

        def masked_stack(u):
            return jnp.concatenate([jnp.where(low3, u, 0.0), jnp.where(low3, 0.0, u)], axis=1)

        cs = slice(c0, c0 + nc)
        ar_s[cs] = jnp.concatenate([a_t, r_t], axis=1).astype(BF16)
        asm_s[cs] = masked_stack(a_t).astype(BF16)
        bkm_s[cs] = jnp.concatenate([masked_stack(b_t), masked_stack(k_t)], axis=1).astype(BF16)
        vsm_s[cs] = masked_stack(v3).astype(BF16)
        v_s[cs] = v3.astype(BF16)
        bhkh_s[cs] = jnp.concatenate([chunked(b_vec) * tail, chunked(kr) * tail], axis=1).astype(BF16)
        eye3 = (lax.broadcasted_iota(jnp.int32, (nc, PAIR, PAIR), 1)
                == lax.broadcasted_iota(jnp.int32, (nc, PAIR, PAIR), 2))
        g_diag = jnp.where(eye3, jnp.broadcast_to(jnp.exp(total), (nc, PAIR, PAIR)), 0.0)
        gcol_s[cs] = jnp.broadcast_to(jnp.sum(g_diag, axis=-1, keepdims=True), (nc, PAIR, PAIR))

    def finish(c0, nc):
        rows = slice(c0 * c_len, (c0 + nc) * c_len)
        y = yraw_s[c0:c0 + nc].reshape(nc * c_len, PAIR)
        mu = _pair_sum(y) * (1.0 / HEAD_DIM)
        d = y - mu
        var = _pair_sum(d * d) * (1.0 / HEAD_DIM)
        yn = d * lax.rsqrt(var + GN_EPS) * prm(P_GN_G) + prm(P_GN_B)
        y_ref[rows, :] = (yn + bonus_s[rows, :]) * gate_s[rows, :]

    row_c = lax.broadcasted_iota(jnp.int32, (c_len, 2 * c_len), 0)
    col_c = lax.broadcasted_iota(jnp.int32, (c_len, 2 * c_len), 1)
    strict0 = (col_c < c_len) & (row_c > col_c)
    strict1 = (col_c >= c_len) & (row_c > col_c - c_len)
    row_w = lax.broadcasted_iota(jnp.int32, (c_len, 4 * c_len), 0)
    col_w = lax.broadcasted_iota(jnp.int32, (c_len, 4 * c_len), 1)
    lower_w = row_w >= (col_w & (c_len - 1))
    same_head = ((lax.broadcasted_iota(jnp.int32, (PAIR, PAIR), 0) < HEAD_DIM)
                 == (lax.broadcasted_iota(jnp.int32, (PAIR, PAIR), 1) < HEAD_DIM))
    eye_f = jnp.where(eye, 1.0, 0.0)
    zeros_cp = jnp.zeros((c_len, PAIR), BF16)
    zeros_2cp = jnp.zeros((2 * c_len, PAIR), BF16)

    def block_diag(x):
        return jnp.concatenate([jnp.where(strict0, x, 0.0), jnp.where(strict1, x, 0.0)], axis=0)

    group = CHUNK_GROUP if n_chunks % CHUNK_GROUP == 0 else 1

    def chunk_terms(cs):
        ars = [ar_s[c] for c in cs]
        ps = [_mm_nt(ar, bkm_s[c]) for ar, c in zip(ars, cs)]
        ls = [block_diag(p[:c_len, :2 * c_len]) for p in ps]
        ts = [eye_f + l for l in ls]
        yield
        l_pows = [_mm(l, l).astype(BF16) for l in ls]
        yield
        n_sq = c_len.bit_length() - 2
        for k in range(1, n_sq + 1):
            if k < n_sq:
                both = [_mm(l, jnp.concatenate([l, t.astype(BF16)], axis=1)) for l, t in zip(l_pows, ts)]
                l_pows = [b[:, :PAIR].astype(BF16) for b in both]
                ts = [t + b[:, PAIR:] for t, b in zip(ts, both)]
            else:
                ts = [t + _mm(l, t) for t, l in zip(ts, l_pows)]
            yield
        vsms = [vsm_s[c] for c in cs]
        lvs = [_mm(block_diag(p[:c_len, 2 * c_len:]), vsm) for p, vsm in zip(ps, vsms)]
        yield
        zs = [_mm(t, jnp.concatenate([asm_s[c], lv.astype(BF16)], axis=1))
              for t, c, lv in zip(ts, cs, lvs)]
        yield
        for c, ar, p, vsm, z in zip(cs, ars, ps, vsms, zs):
            zz = (z[:c_len] + z[c_len:]).astype(BF16)
            rhs4 = jnp.concatenate([zz, jnp.concatenate([zeros_cp, v_s[c]], axis=1)], axis=0)
            gh = _mm_tn(bhkh_s[c], rhs4)
            g_s[c] = jnp.where(same_head, gh[:, :PAIR], 0.0).astype(BF16)
            h_s[c] = jnp.where(same_head, gh[:, PAIR:], 0.0)
            rhs5 = jnp.concatenate(
                [z.astype(BF16), jnp.concatenate([zeros_2cp, vsm], axis=1)], axis=0)
            qy = _mm(jnp.where(lower_w, p[c_len:], 0.0), rhs5)
            q_s[c] = (ar[c_len:].astype(F32) + qy[:, :PAIR]).astype(BF16)
            y0_s[c] = qy[:, PAIR:]

    def recur(c, m):
        mb = m.astype(BF16)
        yraw_s[c] = _mm(q_s[c], mb) + y0_s[c]
        return gcol_s[c] * m + _mm(g_s[c], mb) + h_s[c]

    n_groups = n_chunks // group
    sub = 2 if group % 2 == 0 else 1
    state = [jnp.zeros((PAIR, PAIR), F32)]

    def recur_step(c):
        state[0] = recur(c, state[0])

    def pieces(fn, cg):
        return [functools.partial(fn, cg * group + u, sub) for u in range(0, group, sub)]

    def steps(cg):
        return [functools.partial(recur_step, c) for c in range(cg * group, (cg + 1) * group)]

    for piece in pieces(prepare, 0):
        piece()
    n_segments = c_len.bit_length() + 3
    for cg in range(n_groups + 2):
        side = []
        if cg + 1 < n_groups:
            side.append(pieces(prepare, cg + 1))
        if 1 <= cg <= n_groups:
            side.append(steps(cg - 1))
        if cg >= 2:
            side.append(pieces(finish, cg - 2))
        stages = chunk_terms(list(range(cg * group, (cg + 1) * group))) if cg < n_groups else None
        for k in range(n_segments):
            if stages is not None:
                next(stages, None)
            for work in side:
                for thunk in work[len(work) * k // n_segments:len(work) * (k + 1) // n_segments]:
                    thunk()


def _rwkv(z, v_first, pv, pm, wd, wa, wg, wvu):
    bsz, s_len, _ = z.shape
    first_layer = v_first is None
    n_chunks = s_len // CHUNK

    def col(c0):
        return pl.BlockSpec((None, s_len, PAIR), lambda b, h, c0=c0: (b, 0, c0 + h))

    def fixed(c0):
        return pl.BlockSpec((None, s_len, PAIR), lambda b, h, c0=c0: (b, 0, c0))

    pair_cols = pl.BlockSpec((None, s_len, PAIR), lambda b, h: (b, 0, h))
    w_spec = pl.BlockSpec((PAIR, PAIR), lambda b, h: (0, h))
    in_specs = [col(COL_R), col(COL_KR), col(COL_VR), fixed(COL_DWDA), fixed(COL_DG)]
    args = [z, z, z, z, z]
    if not first_layer:
        in_specs += [fixed(COL_VRES), pair_cols]
        args += [z, v_first]
    in_specs += [pl.BlockSpec((P_ROWS, PAIR), lambda b, h: (0, h)),
                 pl.BlockSpec((8, PAIR), lambda b, h: (0, 0)), w_spec, w_spec, w_spec]
    args += [pv, pm, wd, wa, wg]
    out_shape = [jax.ShapeDtypeStruct((bsz, s_len, WIDTH), F32)]
    out_specs = [pair_cols]
    if first_layer:
        out_shape.append(jax.ShapeDtypeStruct((bsz, s_len, WIDTH), F32))
        out_specs.append(pair_cols)
    else:
        in_specs.append(w_spec)
        args.append(wvu)
    c2, c4 = 2 * CHUNK, 4 * CHUNK
    scratch = [
        pltpu.VMEM((n_chunks, c2, PAIR), BF16),
        pltpu.VMEM((n_chunks, c2, PAIR), BF16),
        pltpu.VMEM((n_chunks, c4, PAIR), BF16),
        pltpu.VMEM((n_chunks, c2, PAIR), BF16),
        pltpu.VMEM((n_chunks, CHUNK, PAIR), BF16),
        pltpu.VMEM((n_chunks, c2, PAIR), BF16),
        pltpu.VMEM((n_chunks, PAIR, PAIR), F32),
        pltpu.VMEM((n_chunks, PAIR, PAIR), BF16),
        pltpu.VMEM((n_chunks, PAIR, PAIR), F32),
        pltpu.VMEM((n_chunks, CHUNK, PAIR), BF16),
        pltpu.VMEM((n_chunks, CHUNK, PAIR), F32),
        pltpu.VMEM((n_chunks, CHUNK, PAIR), F32),
        pltpu.VMEM((s_len, PAIR), F32),
        pltpu.VMEM((s_len, PAIR), F32),
    ]
    out = pl.pallas_call(
        functools.partial(_rwkv_kernel, first_layer),
        grid=(bsz, N_PAIRS),
        in_specs=in_specs,
        out_specs=out_specs,
        out_shape=out_shape,
        scratch_shapes=scratch,
        compiler_params=pltpu.CompilerParams(
            dimension_semantics=("arbitrary", "arbitrary"), vmem_limit_bytes=VMEM_LIMIT),
        name="rwkv_first" if first_layer else "rwkv",
    )(*args)
    return (out[0], out[1]) if first_layer else (out[0], v_first)


def _tail_kernel(x_ref, yf_ref, yr_ref, gf_ref, gr_ref, p_ref, wf_ref, wr_ref, wo_ref, gffn_ref, wu_ref, cw_ref,
                 cb_ref, wd_ref, gple_ref, wpg_ref, wpu_ref, o_ref, h_s, act_s, tail_s):
    rb = x_ref.shape[0]
    n_ff = D_FF // FF_TILE
    first = pl.program_id(1) == 0
    merged = (_sigmoid(gf_ref[...]) * _mm(yf_ref[...], wf_ref[...])
              + _sigmoid(gr_ref[...]) * _mm(yr_ref[...], wr_ref[...]))
    x = x_ref[...] + _mm(merged, wo_ref[...])
    h_s[...] = _rms_rows(x, gffn_ref[...]).astype(BF16)

    def conv(u, tail, c0):
        full = jnp.concatenate([tail, u], axis=0)
        return (cb_ref[:, c0:c0 + FF_TILE] + full[SUBLANES - 2:SUBLANES - 2 + rb] * cw_ref[0:1, c0:c0 + FF_TILE]
                + full[SUBLANES - 1:SUBLANES - 1 + rb] * cw_ref[1:2, c0:c0 + FF_TILE]
                + u * cw_ref[2:3, c0:c0 + FF_TILE])

    def down(j0, j1):
        return jnp.dot(act_s[:, j0 * FF_TILE:j1 * FF_TILE], wd_ref[j0 * FF_TILE:j1 * FF_TILE, :],
                       preferred_element_type=F32)

    half = (n_ff + 1) // 2
    out = x
    for j in range(n_ff):
        cs = []
        for part in range(2):
            c0 = part * D_FF + j * FF_TILE
            u = jnp.dot(h_s[...], wu_ref[:, c0:c0 + FF_TILE], preferred_element_type=F32)
            tail = jnp.where(first, 0.0, tail_s[part, j])
            tail_s[part, j] = u[rb - SUBLANES:]
            cs.append(conv(u, tail, c0))
        c1, c2 = cs
        gelu = 0.5 * c1 * (1.0 + jnp.tanh(c1 * (GELU_C0 + GELU_C1 * (c1 * c1))))
        act_s[:, j * FF_TILE:(j + 1) * FF_TILE] = (gelu * c2).astype(BF16)
        if j == half - 1:
            out = out + down(0, half)
    x = out + down(half, n_ff)
    gate = _sigmoid(_mm(_rms_rows(x, gple_ref[...]), wpg_ref[...]))
    o_ref[...] = x + gate * _mm(p_ref[...], wpu_ref[...])


def _tail(x, y_fox, y_rwkv, z, p, w_of, w_or, w_out, g_ffn, w_up, conv_w, conv_b, w_down, g_ple, w_pg, w_pu):
    bsz, s_len, _ = x.shape
    n_ff = D_FF // FF_TILE
    rb = min(FF_ROWS, s_len)
    assert s_len % rb == 0
    gate_blocks = D_MODEL // LANES

    def rows(width, c0=0):
        return pl.BlockSpec((None, rb, width), lambda b, r, c0=c0: (b, r, c0))

    def whole(a):
        return pl.BlockSpec(a.shape, lambda b, r: (0, 0), pipeline_mode=pl.Buffered(1))

    weights = (w_of, w_or, w_out, g_ffn, w_up, conv_w, conv_b, w_down, g_ple, w_pg, w_pu)
    return pl.pallas_call(
        _tail_kernel,
        grid=(bsz, s_len // rb),
        in_specs=[rows(D_MODEL), rows(WIDTH), rows(WIDTH), rows(D_MODEL, COL_GATE_FOX // gate_blocks),
                  rows(D_MODEL, COL_GATE_RWKV // gate_blocks), rows(PLE_DIM)] + [whole(w) for w in weights],
        out_specs=rows(D_MODEL),
        out_shape=jax.ShapeDtypeStruct(x.shape, F32),
        scratch_shapes=[pltpu.VMEM((rb, D_MODEL), BF16), pltpu.VMEM((rb, D_FF), BF16),
                        pltpu.VMEM((2, n_ff, SUBLANES, FF_TILE), F32)],
        compiler_params=pltpu.CompilerParams(
            dimension_semantics=("arbitrary", "arbitrary"), vmem_limit_bytes=VMEM_LIMIT),
        name="merge_ffn_ple",
    )(x, y_fox, y_rwkv, z, z, p, *weights)


def _pad_cols(a, width):
    return jnp.pad(a, ((0, 0), (0, width - a.shape[1])))


def _pad_rows(a, top, total):
    return jnp.pad(a, ((top, total - top - a.shape[0]), (0, 0)))


def _f_lanes(a):
    trip = jnp.repeat(a, 3, axis=1)
    gap = jnp.zeros((a.shape[0], F_NEG - F_POS - 3 * HEADS), a.dtype)
    return _pad_cols(jnp.concatenate([jnp.zeros((a.shape[0], F_POS), a.dtype), trip, gap, trip], axis=1), LANES)


def _cat_weight(w_in, w_vres_down):
    fox = 3 * WIDTH
    rw0 = fox + HEADS
    lora0 = rw0 + 3 * WIDTH
    gate0 = lora0 + DECAY_LORA + AAA_LORA + GATE_LORA
    vres = (jnp.zeros((D_MODEL, LANES), F32) if w_vres_down is None else _pad_cols(w_vres_down, LANES))
    return jnp.concatenate([
        w_in[:, gate0:], w_in[:, :fox], w_in[:, rw0:lora0], w_in[:, lora0:gate0],
        _f_lanes(w_in[:, fox:rw0]), vres], axis=1).astype(BF16)


def kernel(x, p, g_mix, w_in, b_f, g_qnorm, g_knorm, mu_shift, w_decay_up, w0, w_aaa_up, a0, w_gate_up, k_k, k_a, r_k, gn_g, gn_b, w_vres_down, w_vres_up, v0, w_o_fox, w_o_rwkv, w_out, g_ffn, w_up, conv_w, conv_b, w_down, g_ple, w_ple_gate, w_ple_up):
    bsz, s_len, _ = x.shape
    depth = w_in.shape[0]
    n_tok = bsz * s_len
    assert s_len % ATT_BLOCK == 0 and s_len % CHUNK == 0 and n_tok % min(TOKEN_TILE, n_tok) == 0
    v_first = None
    lane_ids = jnp.arange(LANES)
    lane_mod3 = jnp.where(lane_ids < F_NEG, (lane_ids - F_POS) % 3, (lane_ids - F_NEG) % 3)
    for i in range(depth):
        mu = mu_shift[i]
        zero_row = jnp.zeros((WIDTH,), F32)
        pv = jnp.stack([
            mu[:WIDTH], mu[WIDTH:2 * WIDTH], mu[2 * WIDTH:3 * WIDTH], w0[i], a0[i], k_k[i], k_a[i],
            r_k[i].reshape(WIDTH), gn_g[i], gn_b[i], v0[i - 1] if i else zero_row]
            + [zero_row] * (P_ROWS - 11))
        lora0 = 3 * WIDTH
        pm = jnp.pad(mu[lora0:].reshape(2, PAIR), ((0, 6), (0, 0)))
        wd = _pad_rows(w_decay_up[i], 0, PAIR).astype(BF16)
        wa = _pad_rows(w_aaa_up[i], DECAY_LORA, PAIR).astype(BF16)
        wg = w_gate_up[i].astype(BF16)
        wvu = _pad_rows(w_vres_up[i - 1], 0, PAIR).astype(BF16) if i else None

        ones = jnp.ones((1, HEADS), F32)
        f_prm = jnp.pad(jnp.concatenate([
            _f_lanes(b_f[i][None]), _f_lanes(ones) * (lane_mod3 == 0), _f_lanes(ones) * (lane_mod3 == 1),
            _f_lanes(ones) * jnp.where(lane_ids < F_NEG, LOG2_E, -LOG2_E)]), ((0, SUBLANES - 4), (0, 0)))
        z = _in_proj(x, g_mix[i][None], _cat_weight(w_in[i], w_vres_down[i - 1] if i else None), f_prm)
        y_fox = _fox(z, jnp.tile(g_qnorm[i], 2)[None], jnp.tile(g_knorm[i], 2)[None])
        y_rwkv, v_first = _rwkv(z, v_first, pv, pm, wd, wa, wg, wvu)
        x = _tail(x, y_fox, y_rwkv, z, p[i], w_o_fox[i].astype(BF16), w_o_rwkv[i].astype(BF16),
                  w_out[i].astype(BF16), g_ffn[i][None], w_up[i].astype(BF16), conv_w[i], conv_b[i][None],
                  w_down[i].astype(BF16), g_ple[i][None], w_ple_gate[i].astype(BF16), w_ple_up[i].astype(BF16))
    return x
```

```python
import functools
import math

import jax
import jax.numpy as jnp
from jax import lax
from jax.experimental import pallas as pl
from jax.experimental.pallas import tpu as pltpu

F32 = jnp.float32
BF16 = jnp.bfloat16

LANES = 128
HEAD_DIM = 64
HEADS = 8
PAIR = 2 * HEAD_DIM
N_PAIRS = HEADS // 2
WIDTH = HEADS * HEAD_DIM
D_MODEL = 1024
D_FF = 2816
PLE_DIM = 256
DECAY_LORA = 64
AAA_LORA = 64
GATE_LORA = 128
VRES_LORA = 32
RMS_EPS = 1e-6
GN_EPS = 64e-5
NEG_BIG = -1e30

COL_GATE_FOX = 0
COL_GATE_RWKV = 8
COL_Q = 16
COL_K = 20
COL_V = 24
COL_R = 28
COL_KR = 32
COL_VR = 36
COL_DWDA = 40
COL_DG = 41
COL_F = 42
COL_VRES = 43
N_CAT = 44 * LANES
F_GROUP = 6

IN_TILE = 512
TOKEN_TILE = 512
FF_TILE = 256
FF_ROWS = 512
SUBLANES = 8
DECAY_SCALE = math.exp(-0.5)
LOG2_E = math.log2(math.e)
GELU_C0 = 0.7978845608028654
GELU_C1 = 0.7978845608028654 * 0.044715
CHUNK = 64
CHUNK_GROUP = 8
ATT_BLOCK = 256
VMEM_LIMIT = 56 * 1024 * 1024

(P_MU_R, P_MU_K, P_MU_V, P_W0, P_A0, P_KK, P_KA, P_RK, P_GN_G, P_GN_B, P_V0) = range(11)
P_ROWS = 16


def _mm(a, b):
    return jnp.dot(a.astype(BF16), b.astype(BF16), preferred_element_type=F32)


def _mm_nt(a, b):
    return lax.dot_general(a.astype(BF16), b.astype(BF16), (((1,), (1,)), ((), ())),
                           preferred_element_type=F32)


def _mm_tn(a, b):
    return lax.dot_general(a.astype(BF16), b.astype(BF16), (((0,), (0,)), ((), ())),
                           preferred_element_type=F32)


def _sigmoid(x):
    return 1.0 / (1.0 + jnp.exp(-x))


def _softplus(x):
    return jnp.maximum(x, 0.0) + jnp.log(1.0 + jnp.exp(-jnp.abs(x)))


def _rms_rows(x, g):
    ms = jnp.mean(x * x, axis=-1, keepdims=True)
    return x * lax.rsqrt(ms + RMS_EPS) * g


def _shift_rows(u, k):
    row = lax.broadcasted_iota(jnp.int32, u.shape, 0)
    return jnp.where(row >= k, pltpu.roll(u, k, axis=0), 0.0)


def _cumsum_rows(x, seg):
    pos = lax.broadcasted_iota(jnp.int32, x.shape, 0)
    if seg != x.shape[0]:
        assert seg & (seg - 1) == 0
        pos = pos & (seg - 1)
    k = 1
    while k < seg:
        x = x + jnp.where(pos >= k, pltpu.roll(x, k, axis=0), 0.0)
        k *= 2
    return x


def _pair_sum(x):
    low = lax.broadcasted_iota(jnp.int32, x.shape, 1) < HEAD_DIM
    s0 = jnp.sum(jnp.where(low, x, 0.0), axis=-1, keepdims=True)
    s1 = jnp.sum(jnp.where(low, 0.0, x), axis=-1, keepdims=True)
    return jnp.where(low, s0, s1)


def _in_proj_kernel(x_ref, g_ref, w_ref, fp_ref, z_ref, h_ref):
    @pl.when(pl.program_id(1) == 0)
    def _():
        h_ref[...] = _rms_rows(x_ref[...], g_ref[...]).astype(BF16)

    z_ref[...] = jnp.dot(h_ref[...], w_ref[...], preferred_element_type=F32)

    @pl.when(pl.program_id(1) == COL_F * LANES // IN_TILE)
    def _():
        f0 = COL_F * LANES % IN_TILE
        log_f = -_softplus(-(z_ref[:, f0:f0 + LANES] + fp_ref[0:1, :]))
        hi, mid, lo = _split3(_cumsum_rows(log_f, log_f.shape[0]) * fp_ref[3:4, :])
        z_ref[:, f0:f0 + LANES] = jnp.where(fp_ref[1:2, :] > 0.5, hi, jnp.where(fp_ref[2:3, :] > 0.5, mid, lo))


def _in_proj(x, g, w_cat, f_prm):
    bsz, s_len, _ = x.shape
    return pl.pallas_call(
        _in_proj_kernel,
        grid=(bsz, N_CAT // IN_TILE),
        in_specs=[
            pl.BlockSpec((None, s_len, D_MODEL), lambda b, j: (b, 0, 0)),
            pl.BlockSpec((1, D_MODEL), lambda b, j: (0, 0)),
            pl.BlockSpec((D_MODEL, IN_TILE), lambda b, j: (0, j)),
            pl.BlockSpec((SUBLANES, LANES), lambda b, j: (0, 0)),
        ],
        out_specs=pl.BlockSpec((None, s_len, IN_TILE), lambda b, j: (b, 0, j)),
        out_shape=jax.ShapeDtypeStruct((bsz, s_len, N_CAT), F32),
        scratch_shapes=[pltpu.VMEM((s_len, D_MODEL), BF16)],
        compiler_params=pltpu.CompilerParams(
            dimension_semantics=("arbitrary", "arbitrary"), vmem_limit_bytes=VMEM_LIMIT),
        name="in_proj",
    )(x, g, w_cat, f_prm)


def _split3(c):
    hi = c.astype(BF16).astype(F32)
    r1 = c - hi
    mid = r1.astype(BF16).astype(F32)
    lo = (r1 - mid).astype(BF16).astype(F32)
    return hi, mid, lo


def _fox_kernel(zq_ref, zk_ref, zv_ref, zf_ref, gq_ref, gk_ref, y_ref, q_s, k_s, v_s):
    s_len = zq_ref.shape[0]
    n_blocks = s_len // ATT_BLOCK
    pair = pl.program_id(1)
    lane = lax.broadcasted_iota(jnp.int32, (s_len, PAIR), 1)
    low = lane < HEAD_DIM

    def head_norm(x, g):
        ms = _pair_sum(x * x) * (1.0 / HEAD_DIM)
        return x * lax.rsqrt(ms + RMS_EPS) * g

    q = head_norm(zq_ref[...], gq_ref[...]) * (HEAD_DIM ** -0.5 * LOG2_E)
    k = head_norm(zk_ref[...], gk_ref[...])
    v = zv_ref[...]
    c_parts = zf_ref[...]
    lane_row = lax.broadcasted_iota(jnp.int32, (1, PAIR), 1)
    v_s[...] = v.astype(BF16)

    for hh in range(2):
        base = (HEAD_DIM if hh == 0 else 0) + F_GROUP * pair
        pos = (lane_row >= base) & (lane_row < base + 3)
        neg = (lane_row >= base + 3) & (lane_row < base + 6)
        own = low if hh == 0 else lane >= HEAD_DIM
        q_s[hh] = jnp.where(own, q, jnp.where(pos, c_parts, jnp.where(neg, 1.0, 0.0))).astype(BF16)
        k_s[hh] = jnp.where(own, k, jnp.where(neg, c_parts, jnp.where(pos, 1.0, 0.0))).astype(BF16)

    tri = (lax.broadcasted_iota(jnp.int32, (ATT_BLOCK, ATT_BLOCK), 0)
           >= lax.broadcasted_iota(jnp.int32, (ATT_BLOCK, ATT_BLOCK), 1))
    low_blk = lax.broadcasted_iota(jnp.int32, (ATT_BLOCK, PAIR), 1) < HEAD_DIM

    def attend(hh, i):
        q0 = i * ATT_BLOCK
        qb = q_s[hh, q0:q0 + ATT_BLOCK, :]
        s_diag = jnp.where(tri, _mm_nt(qb, k_s[hh, q0:q0 + ATT_BLOCK, :]), NEG_BIG)
        m = jnp.max(s_diag, axis=-1, keepdims=True)
        if i:
            s_past = _mm_nt(qb, k_s[hh, 0:q0, :])
            m = jnp.maximum(m, jnp.max(s_past, axis=-1, keepdims=True))
        p_diag = jnp.exp2(s_diag - m)
        l = jnp.sum(p_diag, axis=-1, keepdims=True)
        acc = _mm(p_diag, v_s[q0:q0 + ATT_BLOCK, :])
        if i:
            p_past = jnp.exp2(s_past - m)
            l = l + jnp.sum(p_past, axis=-1, keepdims=True)
            acc = acc + _mm(p_past, v_s[0:q0, :])
        return acc / l

    for i in range(n_blocks):
        q0 = i * ATT_BLOCK
        y_ref[q0:q0 + ATT_BLOCK, :] = jnp.where(low_blk, attend(0, i), attend(1, i))


def _fox(z, g_q, g_k):
    bsz, s_len, _ = z.shape

    def col(c0):
        return pl.BlockSpec((None, s_len, PAIR), lambda b, h, c0=c0: (b, 0, c0 + h))

    vec = pl.BlockSpec((1, PAIR), lambda b, h: (0, 0))
    return pl.pallas_call(
        _fox_kernel,
        grid=(bsz, N_PAIRS),
        in_specs=[col(COL_Q), col(COL_K), col(COL_V),
                  pl.BlockSpec((None, s_len, PAIR), lambda b, h: (b, 0, COL_F)), vec, vec],
        out_specs=pl.BlockSpec((None, s_len, PAIR), lambda b, h: (b, 0, h)),
        out_shape=jax.ShapeDtypeStruct((bsz, s_len, WIDTH), F32),
        scratch_shapes=[pltpu.VMEM((2, s_len, PAIR), BF16), pltpu.VMEM((2, s_len, PAIR), BF16),
                        pltpu.VMEM((s_len, PAIR), BF16)],
        compiler_params=pltpu.CompilerParams(
            dimension_semantics=("arbitrary", "arbitrary"), vmem_limit_bytes=VMEM_LIMIT),
        name="fox_attention",
    )(z, z, z, z, g_q, g_k)


def _rwkv_kernel(first_layer, *refs):
    if first_layer:
        (zr_ref, zk_ref, zv_ref, zdwda_ref, zdg_ref, pv_ref, pm_ref, wd_ref, wa_ref, wg_ref,
         y_ref, vfirst_out_ref, *scratch) = refs
    else:
        (zr_ref, zk_ref, zv_ref, zdwda_ref, zdg_ref, zvres_ref, vfirst_ref, pv_ref, pm_ref, wd_ref,
         wa_ref, wg_ref, wvu_ref, y_ref, *scratch) = refs
    (ar_s, asm_s, bkm_s, vsm_s, v_s, bhkh_s, gcol_s, g_s, h_s, q_s, y0_s, yraw_s, bonus_s, gate_s) = scratch

    s_len = zr_ref.shape[0]
    c_len = CHUNK
    n_chunks = s_len // c_len
    eye = (lax.broadcasted_iota(jnp.int32, (PAIR, PAIR), 0) == lax.broadcasted_iota(jnp.int32, (PAIR, PAIR), 1))

    def prm(row):
        return pv_ref[row:row + 1, :]

    def prepare(c0, nc):
        r0, n = c0 * c_len, nc * c_len
        rows = slice(r0, r0 + n)

        def mix(u_ref, mu):
            u = u_ref[rows, :]
            if r0:
                prev = u_ref[r0 - 1:r0 + n - 1, :]
            else:
                prev = jnp.concatenate(
                    [_shift_rows(u_ref[:SUBLANES, :], 1), u_ref[SUBLANES - 1:n - 1, :]], axis=0)
            return u + mu * (prev - u)

        r = mix(zr_ref, prm(P_MU_R))
        kr = mix(zk_ref, prm(P_MU_K))
        vr = mix(zv_ref, prm(P_MU_V))
        dwda = mix(zdwda_ref, pm_ref[0:1, :])
        dg = mix(zdg_ref, pm_ref[1:2, :])

        log_decay = -DECAY_SCALE * _sigmoid(prm(P_W0) + _mm(jnp.tanh(dwda), wd_ref[...]))
        a_gate = _sigmoid(prm(P_A0) + _mm(dwda, wa_ref[...]))
        gate_s[rows, :] = _mm(_sigmoid(dg), wg_ref[...])
        kk = kr * prm(P_KK)
        kk = kk / jnp.maximum(jnp.sqrt(_pair_sum(kk * kk)), 1e-12)
        kr = kr * (1.0 + (a_gate - 1.0) * prm(P_KA))
        if first_layer:
            vfirst_out_ref[rows, :] = vr
        else:
            v_mix = _sigmoid(prm(P_V0) + _mm(zvres_ref[rows, :], wvu_ref[...]))
            vr = vr + (vfirst_ref[rows, :] - vr) * v_mix
        bonus_s[rows, :] = _pair_sum(r * kr * prm(P_RK)) * vr

        def chunked(u):
            return u.reshape(nc, c_len, PAIR)

        cum = chunked(_cumsum_rows(log_decay, c_len))
        total = cum[:, c_len - 1:c_len, :]
        low3 = lax.broadcasted_iota(jnp.int32, (nc, c_len, PAIR), 2) < HEAD_DIM
        b_vec = kk * a_gate
        r_t = chunked(r) * jnp.exp(cum)
        a_t = chunked(-kk) * jnp.exp(cum - chunked(log_decay))
        inv = jnp.exp(-cum)
        b_t = chunked(b_vec) * inv
        k_t = chunked(kr) * inv
        tail = jnp.exp(total - cum)
        v3 = chunked(vr)

        def masked_stack(u):
            return jnp.concatenate([jnp.where(low3, u, 0.0), jnp.where(low3, 0.0, u)], axis=1)

        cs = slice(c0, c0 + nc)
        ar_s[cs] = jnp.concatenate([a_t, r_t], axis=1).astype(BF16)
        asm_s[cs] = masked_stack(a_t).astype(BF16)
        bkm_s[cs] = jnp.concatenate([masked_stack(b_t), masked_stack(k_t)], axis=1).astype(BF16)
        vsm_s[cs] = masked_stack(v3).astype(BF16)
        v_s[cs] = v3.astype(BF16)
        bhkh_s[cs] = jnp.concatenate([chunked(b_vec) * tail, chunked(kr) * tail], axis=1).astype(BF16)
        eye3 = (lax.broadcasted_iota(jnp.int32, (nc, PAIR, PAIR), 1)
                == lax.broadcasted_iota(jnp.int32, (nc, PAIR, PAIR), 2))
        g_diag = jnp.where(eye3, jnp.broadcast_to(jnp.exp(total), (nc, PAIR, PAIR)), 0.0)
        gcol_s[cs] = jnp.broadcast_to(jnp.sum(g_diag, axis=-1, keepdims=True), (nc, PAIR, PAIR))

    def finish(c0, nc):
        rows = slice(c0 * c_len, (c0 + nc) * c_len)
        y = yraw_s[c0:c0 + nc].reshape(nc * c_len, PAIR)
        mu = _pair_sum(y) * (1.0 / HEAD_DIM)
        d = y - mu
        var = _pair_sum(d * d) * (1.0 / HEAD_DIM)
        yn = d * lax.rsqrt(var + GN_EPS) * prm(P_GN_G) + prm(P_GN_B)
        y_ref[rows, :] = (yn + bonus_s[rows, :]) * gate_s[rows, :]

    row_c = lax.broadcasted_iota(jnp.int32, (c_len, 2 * c_len), 0)
    col_c = lax.broadcasted_iota(jnp.int32, (c_len, 2 * c_len), 1)
    strict0 = (col_c < c_len) & (row_c > col_c)
    strict1 = (col_c >= c_len) & (row_c > col_c - c_len)
    row_w = lax.broadcasted_iota(jnp.int32, (c_len, 4 * c_len), 0)
    col_w = lax.broadcasted_iota(jnp.int32, (c_len, 4 * c_len), 1)
    lower_w = row_w >= (col_w & (c_len - 1))
    same_head = ((lax.broadcasted_iota(jnp.int32, (PAIR, PAIR), 0) < HEAD_DIM)
                 == (lax.broadcasted_iota(jnp.int32, (PAIR, PAIR), 1) < HEAD_DIM))
    eye_f = jnp.where(eye, 1.0, 0.0)
    zeros_cp = jnp.zeros((c_len, PAIR), BF16)
    zeros_2cp = jnp.zeros((2 * c_len, PAIR), BF16)

    def block_diag(x):
        return jnp.concatenate([jnp.where(strict0, x, 0.0), jnp.where(strict1, x, 0.0)], axis=0)

    group = CHUNK_GROUP if n_chunks % CHUNK_GROUP == 0 else 1

    def chunk_terms(cs):
        ars = [ar_s[c] for c in cs]
        ps = [_mm_nt(ar, bkm_s[c]) for ar, c in zip(ars, cs)]
        ls = [block_diag(p[:c_len, :2 * c_len]) for p in ps]
        ts = [eye_f + l for l in ls]
        yield
        l_pows = [_mm(l, l).astype(BF16) for l in ls]
        yield
        n_sq = c_len.bit_length() - 2
        for k in range(1, n_sq + 1):
            if k < n_sq:
                both = [_mm(l, jnp.concatenate([l, t.astype(BF16)], axis=1)) for l, t in zip(l_pows, ts)]
                l_pows = [b[:, :PAIR].astype(BF16) for b in both]
                ts = [t + b[:, PAIR:] for t, b in zip(ts, both)]
            else:
                ts = [t + _mm(l, t) for t, l in zip(ts, l_pows)]
            yield
        vsms = [vsm_s[c] for c in cs]
        lvs = [_mm(block_diag(p[:c_len, 2 * c_len:]), vsm) for p, vsm in zip(ps, vsms)]
        yield
        zs = [_mm(t, jnp.concatenate([asm_s[c], lv.astype(BF16)], axis=1))
              for t, c, lv in zip(ts, cs, lvs)]
        yield
        for c, ar, p, vsm, z in zip(cs, ars, ps, vsms, zs):
            zz = (z[:c_len] + z[c_len:]).astype(BF16)
            rhs4 = jnp.concatenate([zz, jnp.concatenate([zeros_cp, v_s[c]], axis=1)], axis=0)
            gh = _mm_tn(bhkh_s[c], rhs4)
            g_s[c] = jnp.where(same_head, gh[:, :PAIR], 0.0).astype(BF16)
            h_s[c] = jnp.where(same_head, gh[:, PAIR:], 0.0)
            rhs5 = jnp.concatenate(
                [z.astype(BF16), jnp.concatenate([zeros_2cp, vsm], axis=1)], axis=0)
            qy = _mm(jnp.where(lower_w, p[c_len:], 0.0), rhs5)
            q_s[c] = (ar[c_len:].astype(F32) + qy[:, :PAIR]).astype(BF16)
            y0_s[c] = qy[:, PAIR:]

    def recur(c, m):
        mb = m.astype(BF16)
        yraw_s[c] = _mm(q_s[c], mb) + y0_s[c]
        return gcol_s[c] * m + _mm(g_s[c], mb) + h_s[c]

    n_groups = n_chunks // group
    sub = 2 if group % 2 == 0 else 1
    state = [jnp.zeros((PAIR, PAIR), F32)]

    def recur_step(c):
        state[0] = recur(c, state[0])

    def pieces(fn, cg):
        return [functools.partial(fn, cg * group + u, sub) for u in range(0, group, sub)]

    def steps(cg):
        return [functools.partial(recur_step, c) for c in range(cg * group, (cg + 1) * group)]

    for piece in pieces(prepare, 0):
        piece()
    n_segments = c_len.bit_length() + 3
    for cg in range(n_groups + 2):
        side = []
        if cg + 1 < n_groups:
            side.append(pieces(prepare, cg + 1))
        if 1 <= cg <= n_groups:
            side.append(steps(cg - 1))
        if cg >= 2:
            side.append(pieces(finish, cg - 2))
        stages = chunk_terms(list(range(cg * group, (cg + 1) * group))) if cg < n_groups else None
        for k in range(n_segments):
            if stages is not None:
                next(stages, None)
            for work in side:
                for thunk in work[len(work) * k // n_segments:len(work) * (k + 1) // n_segments]:
                    thunk()


def _rwkv(z, v_first, pv, pm, wd, wa, wg, wvu):
    bsz, s_len, _ = z.shape
    first_layer = v_first is None
    n_chunks = s_len // CHUNK

    def col(c0):
        return pl.BlockSpec((None, s_len, PAIR), lambda b, h, c0=c0: (b, 0, c0 + h))

    def fixed(c0):
        return pl.BlockSpec((None, s_len, PAIR), lambda b, h, c0=c0: (b, 0, c0))

    pair_cols = pl.BlockSpec((None, s_len, PAIR), lambda b, h: (b, 0, h))
    w_spec = pl.BlockSpec((PAIR, PAIR), lambda b, h: (0, h))
    in_specs = [col(COL_R), col(COL_KR), col(COL_VR), fixed(COL_DWDA), fixed(COL_DG)]
    args = [z, z, z, z, z]
    if not first_layer:
        in_specs += [fixed(COL_VRES), pair_cols]
        args += [z, v_first]
    in_specs += [pl.BlockSpec((P_ROWS, PAIR), lambda b, h: (0, h)),
                 pl.BlockSpec((8, PAIR), lambda b, h: (0, 0)), w_spec, w_spec, w_spec]
    args += [pv, pm, wd, wa, wg]
    out_shape = [jax.ShapeDtypeStruct((bsz, s_len, WIDTH), F32)]
    out_specs = [pair_cols]
    if first_layer:
        out_shape.append(jax.ShapeDtypeStruct((bsz, s_len, WIDTH), F32))
        out_specs.append(pair_cols)
    else:
        in_specs.append(w_spec)
        args.append(wvu)
    c2, c4 = 2 * CHUNK, 4 * CHUNK
    scratch = [
        pltpu.VMEM((n_chunks, c2, PAIR), BF16),
        pltpu.VMEM((n_chunks, c2, PAIR), BF16),
        pltpu.VMEM((n_chunks, c4, PAIR), BF16),
        pltpu.VMEM((n_chunks, c2, PAIR), BF16),
        pltpu.VMEM((n_chunks, CHUNK, PAIR), BF16),
        pltpu.VMEM((n_chunks, c2, PAIR), BF16),
        pltpu.VMEM((n_chunks, PAIR, PAIR), F32),
        pltpu.VMEM((n_chunks, PAIR, PAIR), BF16),
        pltpu.VMEM((n_chunks, PAIR, PAIR), F32),
        pltpu.VMEM((n_chunks, CHUNK, PAIR), BF16),
        pltpu.VMEM((n_chunks, CHUNK, PAIR), F32),
        pltpu.VMEM((n_chunks, CHUNK, PAIR), F32),
        pltpu.VMEM((s_len, PAIR), F32),
        pltpu.VMEM((s_len, PAIR), F32),
    ]
    out = pl.pallas_call(
        functools.partial(_rwkv_kernel, first_layer),
        grid=(bsz, N_PAIRS),
        in_specs=in_specs,
        out_specs=out_specs,
        out_shape=out_shape,
        scratch_shapes=scratch,
        compiler_params=pltpu.CompilerParams(
            dimension_semantics=("arbitrary", "arbitrary"), vmem_limit_bytes=VMEM_LIMIT),
        name="rwkv_first" if first_layer else "rwkv",
    )(*args)
    return (out[0], out[1]) if first_layer else (out[0], v_first)


def _tail_kernel(x_ref, yf_ref, yr_ref, gf_ref, gr_ref, p_ref, wf_ref, wr_ref, wo_ref, gffn_ref, wu_ref, cw_ref,
                 cb_ref, wd_ref, gple_ref, wpg_ref, wpu_ref, o_ref, h_s, act_s, tail_s):
    rb = x_ref.shape[0]
    n_ff = D_FF // FF_TILE
    first = pl.program_id(1) == 0
    merged = (_sigmoid(gf_ref[...]) * _mm(yf_ref[...], wf_ref[...])
              + _sigmoid(gr_ref[...]) * _mm(yr_ref[...], wr_ref[...]))
    x = x_ref[...] + _mm(merged, wo_ref[...])
    h_s[...] = _rms_rows(x, gffn_ref[...]).astype(BF16)

    def conv(u, tail, c0):
        full = jnp.concatenate([tail, u], axis=0)
        return (cb_ref[:, c0:c0 + FF_TILE] + full[SUBLANES - 2:SUBLANES - 2 + rb] * cw_ref[0:1, c0:c0 + FF_TILE]
                + full[SUBLANES - 1:SUBLANES - 1 + rb] * cw_ref[1:2, c0:c0 + FF_TILE]
                + u * cw_ref[2:3, c0:c0 + FF_TILE])

    def down(j0, j1):
        return jnp.dot(act_s[:, j0 * FF_TILE:j1 * FF_TILE], wd_ref[j0 * FF_TILE:j1 * FF_TILE, :],
                       preferred_element_type=F32)

    half = (n_ff + 1) // 2
    out = x
    for j in range(n_ff):
        cs = []
        for part in range(2):
            c0 = part * D_FF + j * FF_TILE
            u = jnp.dot(h_s[...], wu_ref[:, c0:c0 + FF_TILE], preferred_element_type=F32)
            tail = jnp.where(first, 0.0, tail_s[part, j])
            tail_s[part, j] = u[rb - SUBLANES:]
            cs.append(conv(u, tail, c0))
        c1, c2 = cs
        gelu = 0.5 * c1 * (1.0 + jnp.tanh(c1 * (GELU_C0 + GELU_C1 * (c1 * c1))))
        act_s[:, j * FF_TILE:(j + 1) * FF_TILE] = (gelu * c2).astype(BF16)
        if j == half - 1:
            out = out + down(0, half)
    x = out + down(half, n_ff)
    gate = _sigmoid(_mm(_rms_rows(x, gple_ref[...]), wpg_ref[...]))
    o_ref[...] = x + gate * _mm(p_ref[...], wpu_ref[...])


def _tail(x, y_fox, y_rwkv, z, p, w_of, w_or, w_out, g_ffn, w_up, conv_w, conv_b, w_down, g_ple, w_pg, w_pu):
    bsz, s_len, _ = x.shape
    n_ff = D_FF // FF_TILE
    rb = min(FF_ROWS, s_len)
    assert s_len % rb == 0
    gate_blocks = D_MODEL // LANES

    def rows(width, c0=0):
        return pl.BlockSpec((None, rb, width), lambda b, r, c0=c0: (b, r, c0))

    def whole(a):
        return pl.BlockSpec(a.shape, lambda b, r: (0, 0), pipeline_mode=pl.Buffered(1))

    weights = (w_of, w_or, w_out, g_ffn, w_up, conv_w, conv_b, w_down, g_ple, w_pg, w_pu)
    return pl.pallas_call(
        _tail_kernel,
        grid=(bsz, s_len // rb),
        in_specs=[rows(D_MODEL), rows(WIDTH), rows(WIDTH), rows(D_MODEL, COL_GATE_FOX // gate_blocks),
                  rows(D_MODEL, COL_GATE_RWKV // gate_blocks), rows(PLE_DIM)] + [whole(w) for w in weights],
        out_specs=rows(D_MODEL),
        out_shape=jax.ShapeDtypeStruct(x.shape, F32),
        scratch_shapes=[pltpu.VMEM((rb, D_MODEL), BF16), pltpu.VMEM((rb, D_FF), BF16),
                        pltpu.VMEM((2, n_ff, SUBLANES, FF_TILE), F32)],
        compiler_params=pltpu.CompilerParams(
            dimension_semantics=("arbitrary", "arbitrary"), vmem_limit_bytes=VMEM_LIMIT),
        name="merge_ffn_ple",
    )(x, y_fox, y_rwkv, z, z, p, *weights)


def _pad_cols(a, width):
    return jnp.pad(a, ((0, 0), (0, width - a.shape[1])))


def _pad_rows(a, top, total):
    return jnp.pad(a, ((top, total - top - a.shape[0]), (0, 0)))


def _f_lanes(a):
    odd = _pad_cols(jnp.repeat(a[:, 1::2], F_GROUP, axis=1), HEAD_DIM)
    even = _pad_cols(jnp.repeat(a[:, 0::2], F_GROUP, axis=1), HEAD_DIM)
    return jnp.concatenate([odd, even], axis=1)


def _cat_weight(w_in, w_vres_down):
    fox = 3 * WIDTH
    rw0 = fox + HEADS
    lora0 = rw0 + 3 * WIDTH
    gate0 = lora0 + DECAY_LORA + AAA_LORA + GATE_LORA
    vres = (jnp.zeros((D_MODEL, LANES), F32) if w_vres_down is None else _pad_cols(w_vres_down, LANES))
    return jnp.concatenate([
        w_in[:, gate0:], w_in[:, :fox], w_in[:, rw0:lora0], w_in[:, lora0:gate0],
        _f_lanes(w_in[:, fox:rw0]), vres], axis=1).astype(BF16)


def kernel(x, p, g_mix, w_in, b_f, g_qnorm, g_knorm, mu_shift, w_decay_up, w0, w_aaa_up, a0, w_gate_up, k_k, k_a, r_k, gn_g, gn_b, w_vres_down, w_vres_up, v0, w_o_fox, w_o_rwkv, w_out, g_ffn, w_up, conv_w, conv_b, w_down, g_ple, w_ple_gate, w_ple_up):
    bsz, s_len, _ = x.shape
    depth = w_in.shape[0]
    n_tok = bsz * s_len
    assert s_len % ATT_BLOCK == 0 and s_len % CHUNK == 0 and n_tok % min(TOKEN_TILE, n_tok) == 0
    v_first = None
    in_group = (jnp.arange(LANES) % HEAD_DIM) % F_GROUP
    for i in range(depth):
        mu = mu_shift[i]
        zero_row = jnp.zeros((WIDTH,), F32)
        pv = jnp.stack([
            mu[:WIDTH], mu[WIDTH:2 * WIDTH], mu[2 * WIDTH:3 * WIDTH], w0[i], a0[i], k_k[i], k_a[i],
            r_k[i].reshape(WIDTH), gn_g[i], gn_b[i], v0[i - 1] if i else zero_row]
            + [zero_row] * (P_ROWS - 11))
        lora0 = 3 * WIDTH
        pm = jnp.pad(mu[lora0:].reshape(2, PAIR), ((0, 6), (0, 0)))
        wd = _pad_rows(w_decay_up[i], 0, PAIR).astype(BF16)
        wa = _pad_rows(w_aaa_up[i], DECAY_LORA, PAIR).astype(BF16)
        wg = w_gate_up[i].astype(BF16)
        wvu = _pad_rows(w_vres_up[i - 1], 0, PAIR).astype(BF16) if i else None

        ones = jnp.ones((1, HEADS), F32)
        f_prm = jnp.pad(jnp.concatenate([
            _f_lanes(b_f[i][None]), _f_lanes(ones) * (in_group % 3 == 0), _f_lanes(ones) * (in_group % 3 == 1),
            _f_lanes(ones) * jnp.where(in_group < 3, LOG2_E, -LOG2_E)]), ((0, SUBLANES - 4), (0, 0)))
        z = _in_proj(x, g_mix[i][None], _cat_weight(w_in[i], w_vres_down[i - 1] if i else None), f_prm)
        y_fox = _fox(z, jnp.tile(g_qnorm[i], 2)[None], jnp.tile(g_knorm[i], 2)[None])
        y_rwkv, v_first = _rwkv(z, v_first, pv, pm, wd, wa, wg, wvu)
        x = _tail(x, y_fox, y_rwkv, z, p[i], w_o_fox[i].astype(BF16), w_o_rwkv[i].astype(BF16),
                  w_out[i].astype(BF16), g_ffn[i][None], w_up[i].astype(BF16), conv_w[i], conv_b[i][None],
                  w_down[i].astype(BF16), g_ple[i][None], w_ple_gate[i].astype(BF16), w_ple_up[i].astype(BF16))
    return x
```

```python
import functools
import math

import jax
import jax.numpy as jnp
from jax import lax
from jax.experimental import pallas as pl
from jax.experimental.pallas import tpu as pltpu

F32 = jnp.float32
BF16 = jnp.bfloat16

LANES = 128
HEAD_DIM = 64
HEADS = 8
PAIR = 2 * HEAD_DIM
N_PAIRS = HEADS // 2
WIDTH = HEADS * HEAD_DIM
D_MODEL = 1024
D_FF = 2816
PLE_DIM = 256
DECAY_LORA = 64
AAA_LORA = 64
GATE_LORA = 128
VRES_LORA = 32
RMS_EPS = 1e-6
GN_EPS = 64e-5
NEG_BIG = -1e30

COL_GATE_FOX = 0
COL_GATE_RWKV = 8
COL_Q = 16
COL_K = 20
COL_V = 24
COL_R = 28
COL_KR = 32
COL_VR = 36
COL_DWDA = 40
COL_DG = 41
COL_F = 42
COL_VRES = 43
N_CAT = 44 * LANES
F_GROUP = 6

IN_TILE = 512
FF_TILE = 256
FF_ROWS = 512
SUBLANES = 8
DECAY_SCALE = math.exp(-0.5)
LOG2_E = math.log2(math.e)
GELU_C0 = 0.7978845608028654
GELU_C1 = 0.7978845608028654 * 0.044715
CHUNK = 64
CHUNK_GROUP = 8
ATT_BLOCK = 256
VMEM_LIMIT = 56 * 1024 * 1024

(P_MU_R, P_MU_K, P_MU_V, P_W0, P_A0, P_KK, P_KA, P_RK, P_GN_G, P_GN_B, P_V0) = range(11)
P_ROWS = 16


def _mm(a, b):
    return jnp.dot(a.astype(BF16), b.astype(BF16), preferred_element_type=F32)


def _mm_nt(a, b):
    return lax.dot_general(a.astype(BF16), b.astype(BF16), (((1,), (1,)), ((), ())),
                           preferred_element_type=F32)


def _mm_tn(a, b):
    return lax.dot_general(a.astype(BF16), b.astype(BF16), (((0,), (0,)), ((), ())),
                           preferred_element_type=F32)


def _sigmoid(x):
    return 1.0 / (1.0 + jnp.exp(-x))


def _softplus(x):
    return jnp.maximum(x, 0.0) + jnp.log(1.0 + jnp.exp(-jnp.abs(x)))


def _rms_rows(x, g):
    ms = jnp.mean(x * x, axis=-1, keepdims=True)
    return x * lax.rsqrt(ms + RMS_EPS) * g


def _shift_rows(u, k):
    row = lax.broadcasted_iota(jnp.int32, u.shape, 0)
    return jnp.where(row >= k, pltpu.roll(u, k, axis=0), 0.0)


def _cumsum_rows(x, seg):
    pos = lax.broadcasted_iota(jnp.int32, x.shape, 0)
    if seg != x.shape[0]:
        assert seg & (seg - 1) == 0
        pos = pos & (seg - 1)
    k = 1
    while k < seg:
        x = x + jnp.where(pos >= k, pltpu.roll(x, k, axis=0), 0.0)
        k *= 2
    return x


def _pair_sum(x):
    low = lax.broadcasted_iota(jnp.int32, x.shape, 1) < HEAD_DIM
    s0 = jnp.sum(jnp.where(low, x, 0.0), axis=-1, keepdims=True)
    s1 = jnp.sum(jnp.where(low, 0.0, x), axis=-1, keepdims=True)
    return jnp.where(low, s0, s1)


def _in_proj_kernel(x_ref, g_ref, w_ref, fp_ref, z_ref, h_ref):
    @pl.when(pl.program_id(1) == 0)
    def _():
        h_ref[...] = _rms_rows(x_ref[...], g_ref[...]).astype(BF16)

    z_ref[...] = jnp.dot(h_ref[...], w_ref[...], preferred_element_type=F32)

    @pl.when(pl.program_id(1) == COL_F * LANES // IN_TILE)
    def _():
        f0 = COL_F * LANES % IN_TILE
        log_f = -_softplus(-(z_ref[:, f0:f0 + LANES] + fp_ref[0:1, :]))
        hi, mid, lo = _split3(_cumsum_rows(log_f, log_f.shape[0]) * fp_ref[3:4, :])
        z_ref[:, f0:f0 + LANES] = jnp.where(fp_ref[1:2, :] > 0.5, hi, jnp.where(fp_ref[2:3, :] > 0.5, mid, lo))


def _in_proj(x, g, w_cat, f_prm):
    bsz, s_len, _ = x.shape
    return pl.pallas_call(
        _in_proj_kernel,
        grid=(bsz, N_CAT // IN_TILE),
        in_specs=[
            pl.BlockSpec((None, s_len, D_MODEL), lambda b, j: (b, 0, 0)),
            pl.BlockSpec((1, D_MODEL), lambda b, j: (0, 0)),
            pl.BlockSpec((D_MODEL, IN_TILE), lambda b, j: (0, j)),
            pl.BlockSpec((SUBLANES, LANES), lambda b, j: (0, 0)),
        ],
        out_specs=pl.BlockSpec((None, s_len, IN_TILE), lambda b, j: (b, 0, j)),
        out_shape=jax.ShapeDtypeStruct((bsz, s_len, N_CAT), F32),
        scratch_shapes=[pltpu.VMEM((s_len, D_MODEL), BF16)],
        compiler_params=pltpu.CompilerParams(
            dimension_semantics=("arbitrary", "arbitrary"), vmem_limit_bytes=VMEM_LIMIT),
        name="in_proj",
    )(x, g, w_cat, f_prm)


def _split3(c):
    hi = c.astype(BF16).astype(F32)
    r1 = c - hi
    mid = r1.astype(BF16).astype(F32)
    lo = (r1 - mid).astype(BF16).astype(F32)
    return hi, mid, lo


def _fox_kernel(zq_ref, zk_ref, zv_ref, zf_ref, gq_ref, gk_ref, y_ref, q_s, k_s, v_s, o_s):
    s_len = zq_ref.shape[0]
    n_blocks = s_len // ATT_BLOCK
    pair = pl.program_id(1)
    lane = lax.broadcasted_iota(jnp.int32, (ATT_BLOCK, PAIR), 1)
    low = lane < HEAD_DIM
    lane_row = lax.broadcasted_iota(jnp.int32, (1, PAIR), 1)
    tri = (lax.broadcasted_iota(jnp.int32, (ATT_BLOCK, ATT_BLOCK), 0)
           >= lax.broadcasted_iota(jnp.int32, (ATT_BLOCK, ATT_BLOCK), 1))

    def head_norm(x, g):
        ms = _pair_sum(x * x) * (1.0 / HEAD_DIM)
        return x * lax.rsqrt(ms + RMS_EPS) * g

    def prepare(i):
        rows = slice(i * ATT_BLOCK, (i + 1) * ATT_BLOCK)
        q = head_norm(zq_ref[rows, :], gq_ref[...]) * (HEAD_DIM ** -0.5 * LOG2_E)
        k = head_norm(zk_ref[rows, :], gk_ref[...])
        v_s[rows, :] = zv_ref[rows, :].astype(BF16)
        c_parts = zf_ref[rows, :]
        for hh in range(2):
            base = (HEAD_DIM if hh == 0 else 0) + F_GROUP * pair
            pos = (lane_row >= base) & (lane_row < base + 3)
            neg = (lane_row >= base + 3) & (lane_row < base + 6)
            own = low if hh == 0 else lane >= HEAD_DIM
            q_s[hh, rows, :] = jnp.where(own, q, jnp.where(pos, c_parts, jnp.where(neg, 1.0, 0.0))).astype(BF16)
            k_s[hh, rows, :] = jnp.where(own, k, jnp.where(neg, c_parts, jnp.where(pos, 1.0, 0.0))).astype(BF16)

    def attend(hh, i):
        q0 = i * ATT_BLOCK
        rows = slice(q0, q0 + ATT_BLOCK)
        qb = q_s[hh, rows, :]
        s_diag = jnp.where(tri, _mm_nt(qb, k_s[hh, rows, :]), NEG_BIG)
        m = jnp.max(s_diag, axis=-1, keepdims=True)
        if i:
            s_past = _mm_nt(qb, k_s[hh, 0:q0, :])
            m = jnp.maximum(m, jnp.max(s_past, axis=-1, keepdims=True))
        p_diag = jnp.exp2(s_diag - m)
        l = jnp.sum(p_diag, axis=-1, keepdims=True)
        acc = _mm(p_diag, v_s[rows, :])
        if i:
            p_past = jnp.exp2(s_past - m)
            l = l + jnp.sum(p_past, axis=-1, keepdims=True)
            acc = acc + _mm(p_past, v_s[0:q0, :])
        out = acc / l
        if hh == 0:
            o_s[rows, :] = out
        else:
            y_ref[rows, :] = jnp.where(low, o_s[rows, :], out)

    for i in range(n_blocks):
        prepare(i)
        attend(0, i)
        attend(1, i)


def _fox(z, g_q, g_k):
    bsz, s_len, _ = z.shape

    def col(c0):
        return pl.BlockSpec((None, s_len, PAIR), lambda b, h, c0=c0: (b, 0, c0 + h))

    vec = pl.BlockSpec((1, PAIR), lambda b, h: (0, 0))
    return pl.pallas_call(
        _fox_kernel,
        grid=(bsz, N_PAIRS),
        in_specs=[col(COL_Q), col(COL_K), col(COL_V),
                  pl.BlockSpec((None, s_len, PAIR), lambda b, h: (b, 0, COL_F)), vec, vec],
        out_specs=pl.BlockSpec((None, s_len, PAIR), lambda b, h: (b, 0, h)),
        out_shape=jax.ShapeDtypeStruct((bsz, s_len, WIDTH), F32),
        scratch_shapes=[pltpu.VMEM((2, s_len, PAIR), BF16), pltpu.VMEM((2, s_len, PAIR), BF16),
                        pltpu.VMEM((s_len, PAIR), BF16), pltpu.VMEM((s_len, PAIR), F32)],
        compiler_params=pltpu.CompilerParams(
            dimension_semantics=("arbitrary", "arbitrary"), vmem_limit_bytes=VMEM_LIMIT),
        name="fox_attention",
    )(z, z, z, z, g_q, g_k)


def _rwkv_work(first_layer, refs):
    if first_layer:
        (zr_ref, zk_ref, zv_ref, zdwda_ref, zdg_ref, pv_ref, pm_ref, wd_ref, wa_ref, wg_ref,
         y_ref, vfirst_out_ref, *scratch) = refs
    else:
        (zr_ref, zk_ref, zv_ref, zdwda_ref, zdg_ref, zvres_ref, vfirst_ref, pv_ref, pm_ref, wd_ref,
         wa_ref, wg_ref, wvu_ref, y_ref, *scratch) = refs
    (ar_s, asm_s, bkm_s, vsm_s, v_s, bhkh_s, gcol_s, g_s, h_s, q_s, y0_s, yraw_s, bonus_s, gate_s) = scratch

    s_len = zr_ref.shape[0]
    c_len = CHUNK
    n_chunks = s_len // c_len
    eye = (lax.broadcasted_iota(jnp.int32, (PAIR, PAIR), 0) == lax.broadcasted_iota(jnp.int32, (PAIR, PAIR), 1))

    def prm(row):
        return pv_ref[row:row + 1, :]

    def prepare(c0, nc):
        r0, n = c0 * c_len, nc * c_len
        rows = slice(r0, r0 + n)

        def mix(u_ref, mu):
            u = u_ref[rows, :]
            if r0:
                prev = u_ref[r0 - 1:r0 + n - 1, :]
            else:
                prev = jnp.concatenate(
                    [_shift_rows(u_ref[:SUBLANES, :], 1), u_ref[SUBLANES - 1:n - 1, :]], axis=0)
            return u + mu * (prev - u)

        r = mix(zr_ref, prm(P_MU_R))
        kr = mix(zk_ref, prm(P_MU_K))
        vr = mix(zv_ref, prm(P_MU_V))
        dwda = mix(zdwda_ref, pm_ref[0:1, :])
        dg = mix(zdg_ref, pm_ref[1:2, :])

        log_decay = -DECAY_SCALE * _sigmoid(prm(P_W0) + _mm(jnp.tanh(dwda), wd_ref[...]))
        a_gate = _sigmoid(prm(P_A0) + _mm(dwda, wa_ref[...]))
        gate_s[rows, :] = _mm(_sigmoid(dg), wg_ref[...])
        kk = kr * prm(P_KK)
        kk = kk / jnp.maximum(jnp.sqrt(_pair_sum(kk * kk)), 1e-12)
        kr = kr * (1.0 + (a_gate - 1.0) * prm(P_KA))
        if first_layer:
            vfirst_out_ref[rows, :] = vr
        else:
            v_mix = _sigmoid(prm(P_V0) + _mm(zvres_ref[rows, :], wvu_ref[...]))
            vr = vr + (vfirst_ref[rows, :] - vr) * v_mix
        bonus_s[rows, :] = _pair_sum(r * kr * prm(P_RK)) * vr

        def chunked(u):
            return u.reshape(nc, c_len, PAIR)

        cum = chunked(_cumsum_rows(log_decay, c_len))
        total = cum[:, c_len - 1:c_len, :]
        low3 = lax.broadcasted_iota(jnp.int32, (nc, c_len, PAIR), 2) < HEAD_DIM
        b_vec = kk * a_gate
        r_t = chunked(r) * jnp.exp(cum)
        a_t = chunked(-kk) * jnp.exp(cum - chunked(log_decay))
        inv = jnp.exp(-cum)
        b_t = chunked(b_vec) * inv
        k_t = chunked(kr) * inv
        tail = jnp.exp(total - cum)
        v3 = chunked(vr)

        def masked_stack(u):
            return jnp.concatenate([jnp.where(low3, u, 0.0), jnp.where(low3, 0.0, u)], axis=1)

        cs = slice(c0, c0 + nc)
        ar_s[cs] = jnp.concatenate([a_t, r_t], axis=1).astype(BF16)
        asm_s[cs] = masked_stack(a_t).astype(BF16)
        bkm_s[cs] = jnp.concatenate([masked_stack(b_t), masked_stack(k_t)], axis=1).astype(BF16)
        vsm_s[cs] = masked_stack(v3).astype(BF16)
        v_s[cs] = v3.astype(BF16)
        bhkh_s[cs] = jnp.concatenate([chunked(b_vec) * tail, chunked(kr) * tail], axis=1).astype(BF16)
        eye3 = (lax.broadcasted_iota(jnp.int32, (nc, PAIR, PAIR), 1)
                == lax.broadcasted_iota(jnp.int32, (nc, PAIR, PAIR), 2))
        g_diag = jnp.where(eye3, jnp.broadcast_to(jnp.exp(total), (nc, PAIR, PAIR)), 0.0)
        gcol_s[cs] = jnp.broadcast_to(jnp.sum(g_diag, axis=-1, keepdims=True), (nc, PAIR, PAIR))

    def finish(c0, nc):
        rows = slice(c0 * c_len, (c0 + nc) * c_len)
        y = yraw_s[c0:c0 + nc].reshape(nc * c_len, PAIR)
        mu = _pair_sum(y) * (1.0 / HEAD_DIM)
        d = y - mu
        var = _pair_sum(d * d) * (1.0 / HEAD_DIM)
        yn = d * lax.rsqrt(var + GN_EPS) * prm(P_GN_G) + prm(P_GN_B)
        y_ref[rows, :] = (yn + bonus_s[rows, :]) * gate_s[rows, :]

    row_c = lax.broadcasted_iota(jnp.int32, (c_len, 2 * c_len), 0)
    col_c = lax.broadcasted_iota(jnp.int32, (c_len, 2 * c_len), 1)
    strict0 = (col_c < c_len) & (row_c > col_c)
    strict1 = (col_c >= c_len) & (row_c > col_c - c_len)
    row_w = lax.broadcasted_iota(jnp.int32, (c_len, 4 * c_len), 0)
    col_w = lax.broadcasted_iota(jnp.int32, (c_len, 4 * c_len), 1)
    lower_w = row_w >= (col_w & (c_len - 1))
    same_head = ((lax.broadcasted_iota(jnp.int32, (PAIR, PAIR), 0) < HEAD_DIM)
                 == (lax.broadcasted_iota(jnp.int32, (PAIR, PAIR), 1) < HEAD_DIM))
    eye_f = jnp.where(eye, 1.0, 0.0)
    zeros_cp = jnp.zeros((c_len, PAIR), BF16)
    zeros_2cp = jnp.zeros((2 * c_len, PAIR), BF16)

    def block_diag(x):
        return jnp.concatenate([jnp.where(strict0, x, 0.0), jnp.where(strict1, x, 0.0)], axis=0)

    group = CHUNK_GROUP if n_chunks % CHUNK_GROUP == 0 else 1

    def chunk_terms(cs):
        ars = [ar_s[c] for c in cs]
        ps = [_mm_nt(ar, bkm_s[c]) for ar, c in zip(ars, cs)]
        ls = [block_diag(p[:c_len, :2 * c_len]) for p in ps]
        ts = [eye_f + l for l in ls]
        yield
        l_pows = [_mm(l, l).astype(BF16) for l in ls]
        yield
        n_sq = c_len.bit_length() - 2
        for k in range(1, n_sq + 1):
            if k < n_sq:
                both = [_mm(l, jnp.concatenate([l, t.astype(BF16)], axis=1)) for l, t in zip(l_pows, ts)]
                l_pows = [b[:, :PAIR].astype(BF16) for b in both]
                ts = [t + b[:, PAIR:] for t, b in zip(ts, both)]
            else:
                ts = [t + _mm(l, t) for t, l in zip(ts, l_pows)]
            yield
        vsms = [vsm_s[c] for c in cs]
        lvs = [_mm(block_diag(p[:c_len, 2 * c_len:]), vsm) for p, vsm in zip(ps, vsms)]
        yield
        zs = [_mm(t, jnp.concatenate([asm_s[c], lv.astype(BF16)], axis=1))
              for t, c, lv in zip(ts, cs, lvs)]
        yield
        for c, ar, p, vsm, z in zip(cs, ars, ps, vsms, zs):
            zz = (z[:c_len] + z[c_len:]).astype(BF16)
            rhs4 = jnp.concatenate([zz, jnp.concatenate([zeros_cp, v_s[c]], axis=1)], axis=0)
            gh = _mm_tn(bhkh_s[c], rhs4)
            g_s[c] = jnp.where(same_head, gh[:, :PAIR], 0.0).astype(BF16)
            h_s[c] = jnp.where(same_head, gh[:, PAIR:], 0.0)
            rhs5 = jnp.concatenate(
                [z.astype(BF16), jnp.concatenate([zeros_2cp, vsm], axis=1)], axis=0)
            qy = _mm(jnp.where(lower_w, p[c_len:], 0.0), rhs5)
            q_s[c] = (ar[c_len:].astype(F32) + qy[:, :PAIR]).astype(BF16)
            y0_s[c] = qy[:, PAIR:]

    def recur(c, m):
        mb = m.astype(BF16)
        yraw_s[c] = _mm(q_s[c], mb) + y0_s[c]
        return gcol_s[c] * m + _mm(g_s[c], mb) + h_s[c]

    n_groups = n_chunks // group
    sub = 2 if group % 2 == 0 else 1
    state = [jnp.zeros((PAIR, PAIR), F32)]

    def recur_step(c):
        state[0] = recur(c, state[0])

    def pieces(fn, cg):
        return [functools.partial(fn, cg * group + u, sub) for u in range(0, group, sub)]

    def steps(cg):
        return [functools.partial(recur_step, c) for c in range(cg * group, (cg + 1) * group)]

    for piece in pieces(prepare, 0):
        piece()
    n_segments = c_len.bit_length() + 3
    for cg in range(n_groups + 2):
        side = []
        if cg + 1 < n_groups:
            side.append(pieces(prepare, cg + 1))
        if 1 <= cg <= n_groups:
            side.append(steps(cg - 1))
        if cg >= 2:
            side.append(pieces(finish, cg - 2))
        stages = chunk_terms(list(range(cg * group, (cg + 1) * group))) if cg < n_groups else None
        for k in range(n_segments):
            if stages is not None:
                next(stages, None)
            for work in side:
                for thunk in work[len(work) * k // n_segments:len(work) * (k + 1) // n_segments]:
                    thunk()


def _rwkv_kernel(first_layer, *refs):
    _rwkv_work(first_layer, refs)


def _rwkv(z, v_first, pv, pm, wd, wa, wg, wvu):
    bsz, s_len, _ = z.shape
    first_layer = v_first is None
    n_chunks = s_len // CHUNK

    def col(c0):
        return pl.BlockSpec((None, s_len, PAIR), lambda b, h, c0=c0: (b, 0, c0 + h))

    def fixed(c0):
        return pl.BlockSpec((None, s_len, PAIR), lambda b, h, c0=c0: (b, 0, c0))

    pair_cols = pl.BlockSpec((None, s_len, PAIR), lambda b, h: (b, 0, h))
    w_spec = pl.BlockSpec((PAIR, PAIR), lambda b, h: (0, h))
    in_specs = [col(COL_R), col(COL_KR), col(COL_VR), fixed(COL_DWDA), fixed(COL_DG)]
    args = [z, z, z, z, z]
    if not first_layer:
        in_specs += [fixed(COL_VRES), pair_cols]
        args += [z, v_first]
    in_specs += [pl.BlockSpec((P_ROWS, PAIR), lambda b, h: (0, h)),
                 pl.BlockSpec((8, PAIR), lambda b, h: (0, 0)), w_spec, w_spec, w_spec]
    args += [pv, pm, wd, wa, wg]
    out_shape = [jax.ShapeDtypeStruct((bsz, s_len, WIDTH), F32)]
    out_specs = [pair_cols]
    if first_layer:
        out_shape.append(jax.ShapeDtypeStruct((bsz, s_len, WIDTH), F32))
        out_specs.append(pair_cols)
    else:
        in_specs.append(w_spec)
        args.append(wvu)
    c2, c4 = 2 * CHUNK, 4 * CHUNK
    scratch = [
        pltpu.VMEM((n_chunks, c2, PAIR), BF16),
        pltpu.VMEM((n_chunks, c2, PAIR), BF16),
        pltpu.VMEM((n_chunks, c4, PAIR), BF16),
        pltpu.VMEM((n_chunks, c2, PAIR), BF16),
        pltpu.VMEM((n_chunks, CHUNK, PAIR), BF16),
        pltpu.VMEM((n_chunks, c2, PAIR), BF16),
        pltpu.VMEM((n_chunks, PAIR, PAIR), F32),
        pltpu.VMEM((n_chunks, PAIR, PAIR), BF16),
        pltpu.VMEM((n_chunks, PAIR, PAIR), F32),
        pltpu.VMEM((n_chunks, CHUNK, PAIR), BF16),
        pltpu.VMEM((n_chunks, CHUNK, PAIR), F32),
        pltpu.VMEM((n_chunks, CHUNK, PAIR), F32),
        pltpu.VMEM((s_len, PAIR), F32),
        pltpu.VMEM((s_len, PAIR), F32),
    ]
    out = pl.pallas_call(
        functools.partial(_rwkv_kernel, first_layer),
        grid=(bsz, N_PAIRS),
        in_specs=in_specs,
        out_specs=out_specs,
        out_shape=out_shape,
        scratch_shapes=scratch,
        compiler_params=pltpu.CompilerParams(
            dimension_semantics=("arbitrary", "arbitrary"), vmem_limit_bytes=VMEM_LIMIT),
        name="rwkv_first" if first_layer else "rwkv",
    )(*args)
    return (out[0], out[1]) if first_layer else (out[0], v_first)


def _tail_kernel(x_ref, yf_ref, yr_ref, gf_ref, gr_ref, p_ref, wf_ref, wr_ref, wo_ref, gffn_ref, wu_ref, cw_ref,
                 cb_ref, wd_ref, gple_ref, wpg_ref, wpu_ref, o_ref, h_s, act_s, tail_s):
    rb = x_ref.shape[0]
    n_ff = D_FF // FF_TILE
    first = pl.program_id(1) == 0
    merged = (_sigmoid(gf_ref[...]) * _mm(yf_ref[...], wf_ref[...])
              + _sigmoid(gr_ref[...]) * _mm(yr_ref[...], wr_ref[...]))
    x = x_ref[...] + _mm(merged, wo_ref[...])
    h_s[...] = _rms_rows(x, gffn_ref[...]).astype(BF16)

    def conv(u, tail, c0):
        full = jnp.concatenate([tail, u], axis=0)
        return (cb_ref[:, c0:c0 + FF_TILE] + full[SUBLANES - 2:SUBLANES - 2 + rb] * cw_ref[0:1, c0:c0 + FF_TILE]
                + full[SUBLANES - 1:SUBLANES - 1 + rb] * cw_ref[1:2, c0:c0 + FF_TILE]
                + u * cw_ref[2:3, c0:c0 + FF_TILE])

    def down(j0, j1):
        return jnp.dot(act_s[:, j0 * FF_TILE:j1 * FF_TILE], wd_ref[j0 * FF_TILE:j1 * FF_TILE, :],
                       preferred_element_type=F32)

    half = (n_ff + 1) // 2
    out = x
    for j in range(n_ff):
        cs = []
        for part in range(2):
            c0 = part * D_FF + j * FF_TILE
            u = jnp.dot(h_s[...], wu_ref[:, c0:c0 + FF_TILE], preferred_element_type=F32)
            tail = jnp.where(first, 0.0, tail_s[part, j])
            tail_s[part, j] = u[rb - SUBLANES:]
            cs.append(conv(u, tail, c0))
        c1, c2 = cs
        gelu = 0.5 * c1 * (1.0 + jnp.tanh(c1 * (GELU_C0 + GELU_C1 * (c1 * c1))))
        act_s[:, j * FF_TILE:(j + 1) * FF_TILE] = (gelu * c2).astype(BF16)
        if j == half - 1:
            out = out + down(0, half)
    x = out + down(half, n_ff)
    gate = _sigmoid(_mm(_rms_rows(x, gple_ref[...]), wpg_ref[...]))
    o_ref[...] = x + gate * _mm(p_ref[...], wpu_ref[...])


def _tail(x, y_fox, y_rwkv, z, p, w_of, w_or, w_out, g_ffn, w_up, conv_w, conv_b, w_down, g_ple, w_pg, w_pu):
    bsz, s_len, _ = x.shape
    n_ff = D_FF // FF_TILE
    rb = min(FF_ROWS, s_len)
    assert s_len % rb == 0
    gate_blocks = D_MODEL // LANES

    def rows(width, c0=0):
        return pl.BlockSpec((None, rb, width), lambda b, r, c0=c0: (b, r, c0))

    def whole(a):
        return pl.BlockSpec(a.shape, lambda b, r: (0, 0), pipeline_mode=pl.Buffered(1))

    weights = (w_of, w_or, w_out, g_ffn, w_up, conv_w, conv_b, w_down, g_ple, w_pg, w_pu)
    return pl.pallas_call(
        _tail_kernel,
        grid=(bsz, s_len // rb),
        in_specs=[rows(D_MODEL), rows(WIDTH), rows(WIDTH), rows(D_MODEL, COL_GATE_FOX // gate_blocks),
                  rows(D_MODEL, COL_GATE_RWKV // gate_blocks), rows(PLE_DIM)] + [whole(w) for w in weights],
        out_specs=rows(D_MODEL),
        out_shape=jax.ShapeDtypeStruct(x.shape, F32),
        scratch_shapes=[pltpu.VMEM((rb, D_MODEL), BF16), pltpu.VMEM((rb, D_FF), BF16),
                        pltpu.VMEM((2, n_ff, SUBLANES, FF_TILE), F32)],
        compiler_params=pltpu.CompilerParams(
            dimension_semantics=("arbitrary", "arbitrary"), vmem_limit_bytes=VMEM_LIMIT),
        name="merge_ffn_ple",
    )(x, y_fox, y_rwkv, z, z, p, *weights)


def _pad_last(a, width):
    return jnp.pad(a, [(0, 0)] * (a.ndim - 1) + [(0, width - a.shape[-1])])


def _pad_rows(a, top, total):
    return jnp.pad(a, [(0, 0)] * (a.ndim - 2) + [(top, total - top - a.shape[-2]), (0, 0)])


def _f_lanes(a):
    odd = _pad_last(jnp.repeat(a[..., 1::2], F_GROUP, axis=-1), HEAD_DIM)
    even = _pad_last(jnp.repeat(a[..., 0::2], F_GROUP, axis=-1), HEAD_DIM)
    return jnp.concatenate([odd, even], axis=-1)


def _cat_weights(w_in, w_vres_down):
    fox = 3 * WIDTH
    rw0 = fox + HEADS
    lora0 = rw0 + 3 * WIDTH
    gate0 = lora0 + DECAY_LORA + AAA_LORA + GATE_LORA
    vres = jnp.pad(w_vres_down, ((1, 0), (0, 0), (0, LANES - VRES_LORA)))
    return jnp.concatenate([
        w_in[..., gate0:], w_in[..., :fox], w_in[..., rw0:lora0], w_in[..., lora0:gate0],
        _f_lanes(w_in[..., fox:rw0]), vres], axis=-1)


def kernel(x, p, g_mix, w_in, b_f, g_qnorm, g_knorm, mu_shift, w_decay_up, w0, w_aaa_up, a0, w_gate_up, k_k, k_a, r_k, gn_g, gn_b, w_vres_down, w_vres_up, v0, w_o_fox, w_o_rwkv, w_out, g_ffn, w_up, conv_w, conv_b, w_down, g_ple, w_ple_gate, w_ple_up):
    bsz, s_len, _ = x.shape
    depth = w_in.shape[0]
    assert s_len % ATT_BLOCK == 0 and s_len % CHUNK == 0

    def b16(a):
        return a.astype(BF16)

    w_cat = _cat_weights(b16(w_in), b16(w_vres_down))
    w_of, w_or, w_o, w_u, w_d, w_pg, w_pu = map(b16, (w_o_fox, w_o_rwkv, w_out, w_up, w_down, w_ple_gate, w_ple_up))
    wd = b16(_pad_rows(w_decay_up, 0, PAIR))
    wa = b16(_pad_rows(w_aaa_up, DECAY_LORA, PAIR))
    wg = b16(w_gate_up)
    wvu = b16(_pad_rows(w_vres_up, 0, PAIR))
    in_group = (jnp.arange(LANES) % HEAD_DIM) % F_GROUP
    used = _f_lanes(jnp.ones((HEADS,), F32))
    f_rows = jnp.stack([used * (in_group % 3 == 0), used * (in_group % 3 == 1),
                        used * jnp.where(in_group < 3, LOG2_E, -LOG2_E)])
    f_prm = jnp.concatenate([_f_lanes(b_f)[:, None], jnp.broadcast_to(f_rows, (depth, 3, LANES)),
                             jnp.zeros((depth, SUBLANES - 4, LANES), F32)], axis=1)
    mu3 = mu_shift[:, :3 * WIDTH].reshape(depth, 3, WIDTH)
    v0_all = jnp.pad(v0, ((1, 0), (0, 0)))
    pv = jnp.concatenate([
        mu3, jnp.stack([w0, a0, k_k, k_a, r_k.reshape(depth, WIDTH), gn_g, gn_b, v0_all], axis=1),
        jnp.zeros((depth, P_ROWS - 11, WIDTH), F32)], axis=1)
    pm = jnp.pad(mu_shift[:, 3 * WIDTH:].reshape(depth, 2, PAIR), ((0, 0), (0, SUBLANES - 2), (0, 0)))
    g_q = jnp.tile(g_qnorm, (1, 2))[:, None]
    g_k = jnp.tile(g_knorm, (1, 2))[:, None]

    v_first = None
    for i in range(depth):
        z = _in_proj(x, g_mix[i][None], w_cat[i], f_prm[i])
        y_fox = _fox(z, g_q[i], g_k[i])
        y_rwkv, v_first = _rwkv(z, v_first, pv[i], pm[i], wd[i], wa[i], wg[i], wvu[i - 1] if i else None)
        x = _tail(x, y_fox, y_rwkv, z, p[i], w_of[i], w_or[i], w_o[i], g_ffn[i][None], w_u[i], conv_w[i],
                  conv_b[i][None], w_d[i], g_ple[i][None], w_pg[i], w_pu[i])
    return x
```

```python
import functools
import math

import jax
import jax.numpy as jnp
from jax import lax
from jax.experimental import pallas as pl
from jax.experimental.pallas import tpu as pltpu

F32 = jnp.float32
BF16 = jnp.bfloat16

LANES = 128
HEAD_DIM = 64
HEADS = 8
PAIR = 2 * HEAD_DIM
N_PAIRS = HEADS // 2
WIDTH = HEADS * HEAD_DIM
D_MODEL = 1024
D_FF = 2816
PLE_DIM = 256
DECAY_LORA = 64
AAA_LORA = 64
GATE_LORA = 128
VRES_LORA = 32
RMS_EPS = 1e-6
GN_EPS = 64e-5
NEG_BIG = -1e30

COL_GATE_FOX = 0
COL_GATE_RWKV = 8
COL_Q = 16
COL_K = 20
COL_V = 24
COL_R = 28
COL_KR = 32
COL_VR = 36
COL_DWDA = 40
COL_DG = 41
COL_F = 42
COL_VRES = 43
N_CAT = 44 * LANES
F_GROUP = 6

IN_TILE = 512
FF_TILE = 256
FF_ROWS = 512
SUBLANES = 8
DECAY_SCALE = math.exp(-0.5)
LOG2_E = math.log2(math.e)
GELU_C0 = 0.7978845608028654
GELU_C1 = 0.7978845608028654 * 0.044715
CHUNK = 64
CHUNK_GROUP = 8
ATT_BLOCK = 256
VMEM_LIMIT = 56 * 1024 * 1024

(P_MU_R, P_MU_K, P_MU_V, P_W0, P_A0, P_KK, P_KA, P_RK, P_GN_G, P_GN_B, P_V0) = range(11)
P_ROWS = 16


def _mm(a, b):
    return jnp.dot(a.astype(BF16), b.astype(BF16), preferred_element_type=F32)


def _mm_nt(a, b):
    return lax.dot_general(a.astype(BF16), b.astype(BF16), (((1,), (1,)), ((), ())),
                           preferred_element_type=F32)


def _mm_tn(a, b):
    return lax.dot_general(a.astype(BF16), b.astype(BF16), (((0,), (0,)), ((), ())),
                           preferred_element_type=F32)


def _sigmoid(x):
    return 1.0 / (1.0 + jnp.exp(-x))


def _softplus(x):
    return jnp.maximum(x, 0.0) + jnp.log(1.0 + jnp.exp(-jnp.abs(x)))


def _rms_rows(x, g):
    ms = jnp.mean(x * x, axis=-1, keepdims=True)
    return x * lax.rsqrt(ms + RMS_EPS) * g


def _shift_rows(u, k):
    row = lax.broadcasted_iota(jnp.int32, u.shape, 0)
    return jnp.where(row >= k, pltpu.roll(u, k, axis=0), 0.0)


def _cumsum_rows(x, seg):
    pos = lax.broadcasted_iota(jnp.int32, x.shape, 0)
    if seg != x.shape[0]:
        assert seg & (seg - 1) == 0
        pos = pos & (seg - 1)
    k = 1
    while k < seg:
        x = x + jnp.where(pos >= k, pltpu.roll(x, k, axis=0), 0.0)
        k *= 2
    return x


def _pair_sum(x):
    low = lax.broadcasted_iota(jnp.int32, x.shape, 1) < HEAD_DIM
    s0 = jnp.sum(jnp.where(low, x, 0.0), axis=-1, keepdims=True)
    s1 = jnp.sum(jnp.where(low, 0.0, x), axis=-1, keepdims=True)
    return jnp.where(low, s0, s1)


def _in_proj_kernel(x_ref, g_ref, w_ref, fp_ref, z_ref, h_ref):
    @pl.when(pl.program_id(1) == 0)
    def _():
        h_ref[...] = _rms_rows(x_ref[...], g_ref[...]).astype(BF16)

    z_ref[...] = jnp.dot(h_ref[...], w_ref[...], preferred_element_type=F32)

    @pl.when(pl.program_id(1) == COL_F * LANES // IN_TILE)
    def _():
        f0 = COL_F * LANES % IN_TILE
        log_f = -_softplus(-(z_ref[:, f0:f0 + LANES] + fp_ref[0:1, :]))
        hi, mid, lo = _split3(_cumsum_rows(log_f, log_f.shape[0]) * fp_ref[3:4, :])
        z_ref[:, f0:f0 + LANES] = jnp.where(fp_ref[1:2, :] > 0.5, hi, jnp.where(fp_ref[2:3, :] > 0.5, mid, lo))


def _in_proj(layer, x, g, w_cat, f_prm):
    bsz, s_len, _ = x.shape
    return pl.pallas_call(
        _in_proj_kernel,
        grid=(bsz, N_CAT // IN_TILE),
        in_specs=[
            pl.BlockSpec((None, s_len, D_MODEL), lambda b, j: (b, 0, 0)),
            pl.BlockSpec((1, D_MODEL), lambda b, j: (0, 0)),
            pl.BlockSpec((None, D_MODEL, IN_TILE), lambda b, j: (layer, 0, j)),
            pl.BlockSpec((SUBLANES, LANES), lambda b, j: (0, 0)),
        ],
        out_specs=pl.BlockSpec((None, s_len, IN_TILE), lambda b, j: (b, 0, j)),
        out_shape=jax.ShapeDtypeStruct((bsz, s_len, N_CAT), F32),
        scratch_shapes=[pltpu.VMEM((s_len, D_MODEL), BF16)],
        compiler_params=pltpu.CompilerParams(
            dimension_semantics=("arbitrary", "arbitrary"), vmem_limit_bytes=VMEM_LIMIT),
        name="in_proj",
    )(x, g, w_cat, f_prm)


def _split3(c):
    hi = c.astype(BF16).astype(F32)
    r1 = c - hi
    mid = r1.astype(BF16).astype(F32)
    lo = (r1 - mid).astype(BF16).astype(F32)
    return hi, mid, lo


def _fox_kernel(zq_ref, zk_ref, zv_ref, zf_ref, gq_ref, gk_ref, y_ref, q_s, k_s, v_s, o_s):
    s_len = zq_ref.shape[0]
    n_blocks = s_len // ATT_BLOCK
    pair = pl.program_id(1)
    lane = lax.broadcasted_iota(jnp.int32, (ATT_BLOCK, PAIR), 1)
    low = lane < HEAD_DIM
    lane_row = lax.broadcasted_iota(jnp.int32, (1, PAIR), 1)
    tri = (lax.broadcasted_iota(jnp.int32, (ATT_BLOCK, ATT_BLOCK), 0)
           >= lax.broadcasted_iota(jnp.int32, (ATT_BLOCK, ATT_BLOCK), 1))

    def head_norm(x, g):
        ms = _pair_sum(x * x) * (1.0 / HEAD_DIM)
        return x * lax.rsqrt(ms + RMS_EPS) * g

    def prepare(i):
        rows = slice(i * ATT_BLOCK, (i + 1) * ATT_BLOCK)
        q = head_norm(zq_ref[rows, :], gq_ref[...]) * (HEAD_DIM ** -0.5 * LOG2_E)
        k = head_norm(zk_ref[rows, :], gk_ref[...])
        v_s[rows, :] = zv_ref[rows, :].astype(BF16)
        c_parts = zf_ref[rows, :]
        for hh in range(2):
            base = (HEAD_DIM if hh == 0 else 0) + F_GROUP * pair
            pos = (lane_row >= base) & (lane_row < base + 3)
            neg = (lane_row >= base + 3) & (lane_row < base + 6)
            own = low if hh == 0 else lane >= HEAD_DIM
            q_s[hh, rows, :] = jnp.where(own, q, jnp.where(pos, c_parts, jnp.where(neg, 1.0, 0.0))).astype(BF16)
            k_s[hh, rows, :] = jnp.where(own, k, jnp.where(neg, c_parts, jnp.where(pos, 1.0, 0.0))).astype(BF16)

    def attend(hh, i):
        q0 = i * ATT_BLOCK
        rows = slice(q0, q0 + ATT_BLOCK)
        qb = q_s[hh, rows, :]
        s_diag = jnp.where(tri, _mm_nt(qb, k_s[hh, rows, :]), NEG_BIG)
        m = jnp.max(s_diag, axis=-1, keepdims=True)
        if i:
            s_past = _mm_nt(qb, k_s[hh, 0:q0, :])
            m = jnp.maximum(m, jnp.max(s_past, axis=-1, keepdims=True))
        p_diag = jnp.exp2(s_diag - m)
        l = jnp.sum(p_diag, axis=-1, keepdims=True)
        acc = _mm(p_diag, v_s[rows, :])
        if i:
            p_past = jnp.exp2(s_past - m)
            l = l + jnp.sum(p_past, axis=-1, keepdims=True)
            acc = acc + _mm(p_past, v_s[0:q0, :])
        out = acc / l
        if hh == 0:
            o_s[rows, :] = out
        else:
            y_ref[rows, :] = jnp.where(low, o_s[rows, :], out)

    for i in range(n_blocks):
        prepare(i)
        attend(0, i)
        attend(1, i)


def _fox(z, g_q, g_k):
    bsz, s_len, _ = z.shape

    def col(c0):
        return pl.BlockSpec((None, s_len, PAIR), lambda b, h, c0=c0: (b, 0, c0 + h))

    vec = pl.BlockSpec((1, PAIR), lambda b, h: (0, 0))
    return pl.pallas_call(
        _fox_kernel,
        grid=(bsz, N_PAIRS),
        in_specs=[col(COL_Q), col(COL_K), col(COL_V),
                  pl.BlockSpec((None, s_len, PAIR), lambda b, h: (b, 0, COL_F)), vec, vec],
        out_specs=pl.BlockSpec((None, s_len, PAIR), lambda b, h: (b, 0, h)),
        out_shape=jax.ShapeDtypeStruct((bsz, s_len, WIDTH), F32),
        scratch_shapes=[pltpu.VMEM((2, s_len, PAIR), BF16), pltpu.VMEM((2, s_len, PAIR), BF16),
                        pltpu.VMEM((s_len, PAIR), BF16), pltpu.VMEM((s_len, PAIR), F32)],
        compiler_params=pltpu.CompilerParams(
            dimension_semantics=("arbitrary", "arbitrary"), vmem_limit_bytes=VMEM_LIMIT),
        name="fox_attention",
    )(z, z, z, z, g_q, g_k)


def _rwkv_work(first_layer, refs):
    if first_layer:
        (zr_ref, zk_ref, zv_ref, zdwda_ref, zdg_ref, pv_ref, pm_ref, wd_ref, wa_ref, wg_ref,
         y_ref, vfirst_out_ref, *scratch) = refs
    else:
        (zr_ref, zk_ref, zv_ref, zdwda_ref, zdg_ref, zvres_ref, vfirst_ref, pv_ref, pm_ref, wd_ref,
         wa_ref, wg_ref, wvu_ref, y_ref, *scratch) = refs
    (ar_s, asm_s, bkm_s, vsm_s, v_s, bhkh_s, gcol_s, g_s, h_s, q_s, y0_s, yraw_s, bonus_s, gate_s) = scratch

    s_len = zr_ref.shape[0]
    c_len = CHUNK
    n_chunks = s_len // c_len
    eye = (lax.broadcasted_iota(jnp.int32, (PAIR, PAIR), 0) == lax.broadcasted_iota(jnp.int32, (PAIR, PAIR), 1))

    def prm(row):
        return pv_ref[row:row + 1, :]

    def prepare(c0, nc):
        r0, n = c0 * c_len, nc * c_len
        rows = slice(r0, r0 + n)

        def mix(u_ref, mu):
            u = u_ref[rows, :]
            if r0:
                prev = u_ref[r0 - 1:r0 + n - 1, :]
            else:
                prev = jnp.concatenate(
                    [_shift_rows(u_ref[:SUBLANES, :], 1), u_ref[SUBLANES - 1:n - 1, :]], axis=0)
            return u + mu * (prev - u)

        r = mix(zr_ref, prm(P_MU_R))
        kr = mix(zk_ref, prm(P_MU_K))
        vr = mix(zv_ref, prm(P_MU_V))
        dwda = mix(zdwda_ref, pm_ref[0:1, :])
        dg = mix(zdg_ref, pm_ref[1:2, :])

        log_decay = -DECAY_SCALE * _sigmoid(prm(P_W0) + _mm(jnp.tanh(dwda), wd_ref[...]))
        a_gate = _sigmoid(prm(P_A0) + _mm(dwda, wa_ref[...]))
        gate_s[rows, :] = _mm(_sigmoid(dg), wg_ref[...])
        kk = kr * prm(P_KK)
        kk = kk / jnp.maximum(jnp.sqrt(_pair_sum(kk * kk)), 1e-12)
        kr = kr * (1.0 + (a_gate - 1.0) * prm(P_KA))
        if first_layer:
            vfirst_out_ref[rows, :] = vr
        else:
            v_mix = _sigmoid(prm(P_V0) + _mm(zvres_ref[rows, :], wvu_ref[...]))
            vr = vr + (vfirst_ref[rows, :] - vr) * v_mix
        bonus_s[rows, :] = _pair_sum(r * kr * prm(P_RK)) * vr

        def chunked(u):
            return u.reshape(nc, c_len, PAIR)

        cum = chunked(_cumsum_rows(log_decay, c_len))
        total = cum[:, c_len - 1:c_len, :]
        low3 = lax.broadcasted_iota(jnp.int32, (nc, c_len, PAIR), 2) < HEAD_DIM
        b_vec = kk * a_gate
        r_t = chunked(r) * jnp.exp(cum)
        a_t = chunked(-kk) * jnp.exp(cum - chunked(log_decay))
        inv = jnp.exp(-cum)
        b_t = chunked(b_vec) * inv
        k_t = chunked(kr) * inv
        tail = jnp.exp(total - cum)
        v3 = chunked(vr)

        def masked_stack(u):
            return jnp.concatenate([jnp.where(low3, u, 0.0), jnp.where(low3, 0.0, u)], axis=1)

        cs = slice(c0, c0 + nc)
        ar_s[cs] = jnp.concatenate([a_t, r_t], axis=1).astype(BF16)
        asm_s[cs] = masked_stack(a_t).astype(BF16)
        bkm_s[cs] = jnp.concatenate([masked_stack(b_t), masked_stack(k_t)], axis=1).astype(BF16)
        vsm_s[cs] = masked_stack(v3).astype(BF16)
        v_s[cs] = v3.astype(BF16)
        bhkh_s[cs] = jnp.concatenate([chunked(b_vec) * tail, chunked(kr) * tail], axis=1).astype(BF16)
        eye3 = (lax.broadcasted_iota(jnp.int32, (nc, PAIR, PAIR), 1)
                == lax.broadcasted_iota(jnp.int32, (nc, PAIR, PAIR), 2))
        g_diag = jnp.where(eye3, jnp.broadcast_to(jnp.exp(total), (nc, PAIR, PAIR)), 0.0)
        gcol_s[cs] = jnp.broadcast_to(jnp.sum(g_diag, axis=-1, keepdims=True), (nc, PAIR, PAIR))

    def finish(c0, nc):
        rows = slice(c0 * c_len, (c0 + nc) * c_len)
        y = yraw_s[c0:c0 + nc].reshape(nc * c_len, PAIR)
        mu = _pair_sum(y) * (1.0 / HEAD_DIM)
        d = y - mu
        var = _pair_sum(d * d) * (1.0 / HEAD_DIM)
        yn = d * lax.rsqrt(var + GN_EPS) * prm(P_GN_G) + prm(P_GN_B)
        y_ref[rows, :] = (yn + bonus_s[rows, :]) * gate_s[rows, :]

    row_c = lax.broadcasted_iota(jnp.int32, (c_len, 2 * c_len), 0)
    col_c = lax.broadcasted_iota(jnp.int32, (c_len, 2 * c_len), 1)
    strict0 = (col_c < c_len) & (row_c > col_c)
    strict1 = (col_c >= c_len) & (row_c > col_c - c_len)
    row_w = lax.broadcasted_iota(jnp.int32, (c_len, 4 * c_len), 0)
    col_w = lax.broadcasted_iota(jnp.int32, (c_len, 4 * c_len), 1)
    lower_w = row_w >= (col_w & (c_len - 1))
    same_head = ((lax.broadcasted_iota(jnp.int32, (PAIR, PAIR), 0) < HEAD_DIM)
                 == (lax.broadcasted_iota(jnp.int32, (PAIR, PAIR), 1) < HEAD_DIM))
    eye_f = jnp.where(eye, 1.0, 0.0)
    zeros_cp = jnp.zeros((c_len, PAIR), BF16)
    zeros_2cp = jnp.zeros((2 * c_len, PAIR), BF16)

    def block_diag(x):
        return jnp.concatenate([jnp.where(strict0, x, 0.0), jnp.where(strict1, x, 0.0)], axis=0)

    group = CHUNK_GROUP if n_chunks % CHUNK_GROUP == 0 else 1

    def chunk_terms(cs):
        ars = [ar_s[c] for c in cs]
        ps = [_mm_nt(ar, bkm_s[c]) for ar, c in zip(ars, cs)]
        ls = [block_diag(p[:c_len, :2 * c_len]) for p in ps]
        ts = [eye_f + l for l in ls]
        yield
        l_pows = [_mm(l, l).astype(BF16) for l in ls]
        yield
        n_sq = c_len.bit_length() - 2
        for k in range(1, n_sq + 1):
            if k < n_sq:
                both = [_mm(l, jnp.concatenate([l, t.astype(BF16)], axis=1)) for l, t in zip(l_pows, ts)]
                l_pows = [b[:, :PAIR].astype(BF16) for b in both]
                ts = [t + b[:, PAIR:] for t, b in zip(ts, both)]
            else:
                ts = [t + _mm(l, t) for t, l in zip(ts, l_pows)]
            yield
        vsms = [vsm_s[c] for c in cs]
        lvs = [_mm(block_diag(p[:c_len, 2 * c_len:]), vsm) for p, vsm in zip(ps, vsms)]
        yield
        zs = [_mm(t, jnp.concatenate([asm_s[c], lv.astype(BF16)], axis=1))
              for t, c, lv in zip(ts, cs, lvs)]
        yield
        for c, ar, p, vsm, z in zip(cs, ars, ps, vsms, zs):
            zz = (z[:c_len] + z[c_len:]).astype(BF16)
            rhs4 = jnp.concatenate([zz, jnp.concatenate([zeros_cp, v_s[c]], axis=1)], axis=0)
            gh = _mm_tn(bhkh_s[c], rhs4)
            g_s[c] = jnp.where(same_head, gh[:, :PAIR], 0.0).astype(BF16)
            h_s[c] = jnp.where(same_head, gh[:, PAIR:], 0.0)
            rhs5 = jnp.concatenate(
                [z.astype(BF16), jnp.concatenate([zeros_2cp, vsm], axis=1)], axis=0)
            qy = _mm(jnp.where(lower_w, p[c_len:], 0.0), rhs5)
            q_s[c] = (ar[c_len:].astype(F32) + qy[:, :PAIR]).astype(BF16)
            y0_s[c] = qy[:, PAIR:]

    def recur(c, m):
        mb = m.astype(BF16)
        yraw_s[c] = _mm(q_s[c], mb) + y0_s[c]
        return gcol_s[c] * m + _mm(g_s[c], mb) + h_s[c]

    n_groups = n_chunks // group
    sub = 2 if group % 2 == 0 else 1
    state = [jnp.zeros((PAIR, PAIR), F32)]

    def recur_step(c):
        state[0] = recur(c, state[0])

    def pieces(fn, cg):
        return [functools.partial(fn, cg * group + u, sub) for u in range(0, group, sub)]

    def steps(cg):
        return [functools.partial(recur_step, c) for c in range(cg * group, (cg + 1) * group)]

    for piece in pieces(prepare, 0):
        piece()
    n_segments = c_len.bit_length() + 3
    for cg in range(n_groups + 2):
        side = []
        if cg + 1 < n_groups:
            side.append(pieces(prepare, cg + 1))
        if 1 <= cg <= n_groups:
            side.append(steps(cg - 1))
        if cg >= 2:
            side.append(pieces(finish, cg - 2))
        stages = chunk_terms(list(range(cg * group, (cg + 1) * group))) if cg < n_groups else None
        for k in range(n_segments):
            if stages is not None:
                next(stages, None)
            for work in side:
                for thunk in work[len(work) * k // n_segments:len(work) * (k + 1) // n_segments]:
                    thunk()


def _rwkv_kernel(first_layer, *refs):
    _rwkv_work(first_layer, refs)


def _rwkv(z, v_first, pv, pm, wd, wa, wg, wvu):
    bsz, s_len, _ = z.shape
    first_layer = v_first is None
    n_chunks = s_len // CHUNK

    def col(c0):
        return pl.BlockSpec((None, s_len, PAIR), lambda b, h, c0=c0: (b, 0, c0 + h))

    def fixed(c0):
        return pl.BlockSpec((None, s_len, PAIR), lambda b, h, c0=c0: (b, 0, c0))

    pair_cols = pl.BlockSpec((None, s_len, PAIR), lambda b, h: (b, 0, h))
    w_spec = pl.BlockSpec((PAIR, PAIR), lambda b, h: (0, h))
    in_specs = [col(COL_R), col(COL_KR), col(COL_VR), fixed(COL_DWDA), fixed(COL_DG)]
    args = [z, z, z, z, z]
    if not first_layer:
        in_specs += [fixed(COL_VRES), pair_cols]
        args += [z, v_first]
    in_specs += [pl.BlockSpec((P_ROWS, PAIR), lambda b, h: (0, h)),
                 pl.BlockSpec((8, PAIR), lambda b, h: (0, 0)), w_spec, w_spec, w_spec]
    args += [pv, pm, wd, wa, wg]
    out_shape = [jax.ShapeDtypeStruct((bsz, s_len, WIDTH), F32)]
    out_specs = [pair_cols]
    if first_layer:
        out_shape.append(jax.ShapeDtypeStruct((bsz, s_len, WIDTH), F32))
        out_specs.append(pair_cols)
    else:
        in_specs.append(w_spec)
        args.append(wvu)
    c2, c4 = 2 * CHUNK, 4 * CHUNK
    scratch = [
        pltpu.VMEM((n_chunks, c2, PAIR), BF16),
        pltpu.VMEM((n_chunks, c2, PAIR), BF16),
        pltpu.VMEM((n_chunks, c4, PAIR), BF16),
        pltpu.VMEM((n_chunks, c2, PAIR), BF16),
        pltpu.VMEM((n_chunks, CHUNK, PAIR), BF16),
        pltpu.VMEM((n_chunks, c2, PAIR), BF16),
        pltpu.VMEM((n_chunks, PAIR, PAIR), F32),
        pltpu.VMEM((n_chunks, PAIR, PAIR), BF16),
        pltpu.VMEM((n_chunks, PAIR, PAIR), F32),
        pltpu.VMEM((n_chunks, CHUNK, PAIR), BF16),
        pltpu.VMEM((n_chunks, CHUNK, PAIR), F32),
        pltpu.VMEM((n_chunks, CHUNK, PAIR), F32),
        pltpu.VMEM((s_len, PAIR), F32),
        pltpu.VMEM((s_len, PAIR), F32),
    ]
    out = pl.pallas_call(
        functools.partial(_rwkv_kernel, first_layer),
        grid=(bsz, N_PAIRS),
        in_specs=in_specs,
        out_specs=out_specs,
        out_shape=out_shape,
        scratch_shapes=scratch,
        compiler_params=pltpu.CompilerParams(
            dimension_semantics=("arbitrary", "arbitrary"), vmem_limit_bytes=VMEM_LIMIT),
        name="rwkv_first" if first_layer else "rwkv",
    )(*args)
    return (out[0], out[1]) if first_layer else (out[0], v_first)


def _tail_kernel(x_ref, yf_ref, yr_ref, gf_ref, gr_ref, p_ref, wf_ref, wr_ref, wo_ref, gffn_ref, wu_ref, cw_ref,
                 cb_ref, wd_ref, gple_ref, wpg_ref, wpu_ref, o_ref, h_s, act_s, tail_s):
    rb = x_ref.shape[0]
    n_ff = D_FF // FF_TILE
    first = pl.program_id(1) == 0
    merged = (_sigmoid(gf_ref[...]) * _mm(yf_ref[...], wf_ref[...])
              + _sigmoid(gr_ref[...]) * _mm(yr_ref[...], wr_ref[...]))
    x = x_ref[...] + _mm(merged, wo_ref[...])
    h_s[...] = _rms_rows(x, gffn_ref[...]).astype(BF16)

    def conv(u, tail, c0):
        full = jnp.concatenate([tail, u], axis=0)
        return (cb_ref[:, c0:c0 + FF_TILE] + full[SUBLANES - 2:SUBLANES - 2 + rb] * cw_ref[0:1, c0:c0 + FF_TILE]
                + full[SUBLANES - 1:SUBLANES - 1 + rb] * cw_ref[1:2, c0:c0 + FF_TILE]
                + u * cw_ref[2:3, c0:c0 + FF_TILE])

    def down(j0, j1):
        return jnp.dot(act_s[:, j0 * FF_TILE:j1 * FF_TILE], wd_ref[j0 * FF_TILE:j1 * FF_TILE, :],
                       preferred_element_type=F32)

    half = (n_ff + 1) // 2
    out = x
    for j in range(n_ff):
        cs = []
        for part in range(2):
            c0 = part * D_FF + j * FF_TILE
            u = jnp.dot(h_s[...], wu_ref[:, c0:c0 + FF_TILE], preferred_element_type=F32)
            tail = jnp.where(first, 0.0, tail_s[part, j])
            tail_s[part, j] = u[rb - SUBLANES:]
            cs.append(conv(u, tail, c0))
        c1, c2 = cs
        gelu = 0.5 * c1 * (1.0 + jnp.tanh(c1 * (GELU_C0 + GELU_C1 * (c1 * c1))))
        act_s[:, j * FF_TILE:(j + 1) * FF_TILE] = (gelu * c2).astype(BF16)
        if j == half - 1:
            out = out + down(0, half)
    x = out + down(half, n_ff)
    gate = _sigmoid(_mm(_rms_rows(x, gple_ref[...]), wpg_ref[...]))
    o_ref[...] = x + gate * _mm(p_ref[...], wpu_ref[...])


def _tail(layer, x, y_fox, y_rwkv, z, p, w_of, w_or, w_out, g_ffn, w_up, conv_w, conv_b, w_down, g_ple, w_pg, w_pu):
    bsz, s_len, _ = x.shape
    n_ff = D_FF // FF_TILE
    rb = min(FF_ROWS, s_len)
    assert s_len % rb == 0
    gate_blocks = D_MODEL // LANES

    def rows(width, c0=0):
        return pl.BlockSpec((None, rb, width), lambda b, r, c0=c0: (b, r, c0))

    def whole(a):
        return pl.BlockSpec((None,) + a.shape[1:], lambda b, r: (layer, 0, 0), pipeline_mode=pl.Buffered(1))

    weights = (w_of, w_or, w_out, g_ffn, w_up, conv_w, conv_b, w_down, g_ple, w_pg, w_pu)
    return pl.pallas_call(
        _tail_kernel,
        grid=(bsz, s_len // rb),
        in_specs=[rows(D_MODEL), rows(WIDTH), rows(WIDTH), rows(D_MODEL, COL_GATE_FOX // gate_blocks),
                  rows(D_MODEL, COL_GATE_RWKV // gate_blocks),
                  pl.BlockSpec((None, None, rb, PLE_DIM), lambda b, r: (layer, b, r, 0))]
        + [whole(w) for w in weights],
        out_specs=rows(D_MODEL),
        out_shape=jax.ShapeDtypeStruct(x.shape, F32),
        scratch_shapes=[pltpu.VMEM((rb, D_MODEL), BF16), pltpu.VMEM((rb, D_FF), BF16),
                        pltpu.VMEM((2, n_ff, SUBLANES, FF_TILE), F32)],
        compiler_params=pltpu.CompilerParams(
            dimension_semantics=("arbitrary", "arbitrary"), vmem_limit_bytes=VMEM_LIMIT),
        name="merge_ffn_ple",
    )(x, y_fox, y_rwkv, z, z, p, *weights)


def _pad_last(a, width):
    return jnp.pad(a, [(0, 0)] * (a.ndim - 1) + [(0, width - a.shape[-1])])


def _pad_rows(a, top, total):
    return jnp.pad(a, [(0, 0)] * (a.ndim - 2) + [(top, total - top - a.shape[-2]), (0, 0)])


def _f_lanes(a):
    odd = _pad_last(jnp.repeat(a[..., 1::2], F_GROUP, axis=-1), HEAD_DIM)
    even = _pad_last(jnp.repeat(a[..., 0::2], F_GROUP, axis=-1), HEAD_DIM)
    return jnp.concatenate([odd, even], axis=-1)


def _cat_weights(w_in, w_vres_down):
    fox = 3 * WIDTH
    rw0 = fox + HEADS
    lora0 = rw0 + 3 * WIDTH
    gate0 = lora0 + DECAY_LORA + AAA_LORA + GATE_LORA
    vres = jnp.pad(w_vres_down, ((1, 0), (0, 0), (0, LANES - VRES_LORA)))
    return jnp.concatenate([
        w_in[..., gate0:], w_in[..., :fox], w_in[..., rw0:lora0], w_in[..., lora0:gate0],
        _f_lanes(w_in[..., fox:rw0]), vres], axis=-1)


def kernel(x, p, g_mix, w_in, b_f, g_qnorm, g_knorm, mu_shift, w_decay_up, w0, w_aaa_up, a0, w_gate_up, k_k, k_a, r_k, gn_g, gn_b, w_vres_down, w_vres_up, v0, w_o_fox, w_o_rwkv, w_out, g_ffn, w_up, conv_w, conv_b, w_down, g_ple, w_ple_gate, w_ple_up):
    bsz, s_len, _ = x.shape
    depth = w_in.shape[0]
    assert s_len % ATT_BLOCK == 0 and s_len % CHUNK == 0

    def b16(a):
        return a.astype(BF16)

    w_cat = b16(_cat_weights(w_in, w_vres_down))
    w_of, w_or, w_o, w_u, w_d, w_pg, w_pu = map(b16, (w_o_fox, w_o_rwkv, w_out, w_up, w_down, w_ple_gate, w_ple_up))
    wd = b16(_pad_rows(w_decay_up, 0, PAIR))
    wa = b16(_pad_rows(w_aaa_up, DECAY_LORA, PAIR))
    wg = b16(w_gate_up)
    wvu = b16(_pad_rows(w_vres_up, 0, PAIR))
    in_group = (jnp.arange(LANES) % HEAD_DIM) % F_GROUP
    used = _f_lanes(jnp.ones((HEADS,), F32))
    f_rows = jnp.stack([used * (in_group % 3 == 0), used * (in_group % 3 == 1),
                        used * jnp.where(in_group < 3, LOG2_E, -LOG2_E)])
    f_prm = jnp.concatenate([_f_lanes(b_f)[:, None], jnp.broadcast_to(f_rows, (depth, 3, LANES)),
                             jnp.zeros((depth, SUBLANES - 4, LANES), F32)], axis=1)
    mu3 = mu_shift[:, :3 * WIDTH].reshape(depth, 3, WIDTH)
    v0_all = jnp.pad(v0, ((1, 0), (0, 0)))
    pv = jnp.concatenate([
        mu3, jnp.stack([w0, a0, k_k, k_a, r_k.reshape(depth, WIDTH), gn_g, gn_b, v0_all], axis=1),
        jnp.zeros((depth, P_ROWS - 11, WIDTH), F32)], axis=1)
    pm = jnp.pad(mu_shift[:, 3 * WIDTH:].reshape(depth, 2, PAIR), ((0, 0), (0, SUBLANES - 2), (0, 0)))
    g_q = jnp.tile(g_qnorm, (1, 2))[:, None]
    g_k = jnp.tile(g_knorm, (1, 2))[:, None]

    v_first = None
    for i in range(depth):
        z = _in_proj(i, x, g_mix[i][None], w_cat, f_prm[i])
        y_fox = _fox(z, g_q[i], g_k[i])
        y_rwkv, v_first = _rwkv(z, v_first, pv[i], pm[i], wd[i], wa[i], wg[i], wvu[i - 1] if i else None)
        x = _tail(i, x, y_fox, y_rwkv, z, p, w_of, w_or, w_o, g_ffn[:, None], w_u, conv_w, conv_b[:, None],
                  w_d, g_ple[:, None], w_pg, w_pu)
    return x
```
